```python
import jax
import jax.numpy as jnp
from jax import lax
import numpy as np

D_MODEL = 2048
BATCH = 1
SEQ = 8192
DEPTH = 1

CHUNK = 64
RMS_EPS = 1e-6
GLA_HEADS = 4
GLA_DK = 256
GLA_DV = 512
GLA_GATE_RANK = 16
GLA_GATE_TAU = 16.0
GLA_QK = GLA_HEADS * GLA_DK
GLA_V = GLA_HEADS * GLA_DV
ATT_HEADS = 8
ATT_HD = 128
ATT_W = ATT_HEADS * ATT_HD
ATT_LEFT_CHUNKS = 8
ATT_BAND = ATT_LEFT_CHUNKS + 1
REL_CLIP = 128
N_EXPERTS = 32
TOP_K = 4
D_FF = 2048
SWIGLU_LIMIT = 7.0
SWIGLU_ALPHA = 1.702
MOE_BLOCK = 128
SPLITS = (GLA_QK, GLA_QK, GLA_V, GLA_V, GLA_GATE_RANK, ATT_W, ATT_W, ATT_W, D_MODEL, D_MODEL)
IN_COLS = 2 * GLA_QK + 2 * GLA_V + GLA_GATE_RANK + 3 * ATT_W + 2 * D_MODEL

kernel_name = 'hybrid_gla_chunkattn_moe_block'


def rms_norm(x, g):
    xf = x.astype(jnp.float32)
    y = xf * lax.rsqrt(jnp.mean(xf * xf, axis=-1, keepdims=True) + RMS_EPS)
    return (y * g.astype(jnp.float32)).astype(x.dtype)


def gla_mixer(q, k, v, r, gate_lr, w_gate_up, b_gate, out_g):
    B, S, _ = q.shape
    NC = S // CHUNK
    f32 = jnp.float32

    def to_chunks(t, d):
        return t.astype(f32).reshape(B, NC, CHUNK, GLA_HEADS, d).transpose(0, 1, 3, 2, 4)

    log_a = jax.nn.log_sigmoid((gate_lr @ w_gate_up + b_gate).astype(f32)) / GLA_GATE_TAU
    bcum = jnp.cumsum(to_chunks(log_a, GLA_DK), axis=3)
    b_last = bcum[:, :, :, -1:, :]
    qc = to_chunks(q, GLA_DK) * (GLA_DK ** -0.5)
    kc = to_chunks(k, GLA_DK)
    vc = to_chunks(v, GLA_DV)
    q_dec = qc * jnp.exp(bcum)
    k_inv = kc * jnp.exp(-bcum)
    k_end = kc * jnp.exp(b_last - bcum)
    causal = np.tril(np.ones((CHUNK, CHUNK), dtype=bool))
    a_intra = jnp.where(causal, jnp.einsum('bnhik,bnhjk->bnhij', q_dec, k_inv), 0.0)
    o_intra = jnp.einsum('bnhij,bnhjv->bnhiv', a_intra, vc)

    def step(state, inp):
        qd, ke, vv, dec = inp
        o_inter = jnp.einsum('bhik,bhkv->bhiv', qd, state)
        state = state * dec[..., None] + jnp.einsum('bhjk,bhjv->bhkv', ke, vv)
        return state, o_inter

    xs = (jnp.moveaxis(q_dec, 1, 0), jnp.moveaxis(k_end, 1, 0), jnp.moveaxis(vc, 1, 0),
          jnp.moveaxis(jnp.exp(b_last[:, :, :, 0, :]), 1, 0))
    s0 = jnp.zeros((B, GLA_HEADS, GLA_DK, GLA_DV), f32)
    _, o_inter = lax.scan(step, s0, xs)
    o = o_intra + jnp.moveaxis(o_inter, 0, 1)
    o = o.transpose(0, 1, 3, 2, 4).reshape(B, S, GLA_HEADS, GLA_DV)
    o = rms_norm(o, out_g).reshape(B, S, GLA_V)
    return (o * jax.nn.silu(r.astype(f32))).astype(q.dtype)


def chunk_attention(q, k, v, q_g, k_g, rel_bias):
    B, S, _ = q.shape
    NC = S // CHUNK
    q = rms_norm(q.reshape(B, S, ATT_HEADS, ATT_HD), q_g)
    k = rms_norm(k.reshape(B, S, ATT_HEADS, ATT_HD), k_g)
    v = v.reshape(B, S, ATT_HEADS, ATT_HD)
    qc = q.reshape(B, NC, CHUNK, ATT_HEADS, ATT_HD)

    def band(t):
        tp = jnp.pad(t.reshape(B, NC, CHUNK, ATT_HEADS, ATT_HD),
                     ((0, 0), (ATT_LEFT_CHUNKS, 0), (0, 0), (0, 0), (0, 0)))
        return jnp.stack([tp[:, w:w + NC] for w in range(ATT_BAND)], axis=2).reshape(
            B, NC, ATT_BAND * CHUNK, ATT_HEADS, ATT_HD)

    kb = band(k)
    vb = band(v)
    scores = jnp.einsum('bnihd,bnjhd->bnhij', qc, kb).astype(jnp.float32) * (ATT_HD ** -0.5)
    iq = np.arange(CHUNK)[:, None]
    jk = np.arange(ATT_BAND * CHUNK)[None, :] - ATT_LEFT_CHUNKS * CHUNK
    rel_idx = np.clip(iq - jk, -REL_CLIP, REL_CLIP) + REL_CLIP
    bias = rel_bias.astype(jnp.float32)[:, rel_idx]
    valid = (np.arange(NC)[:, None] - ATT_LEFT_CHUNKS
             + np.arange(ATT_BAND * CHUNK)[None, :] // CHUNK) >= 0
    scores = jnp.where(valid[None, :, None, None, :], scores + bias[None, None],
                       jnp.finfo(jnp.float32).min)
    p = jax.nn.softmax(scores, axis=-1).astype(v.dtype)
    o = jnp.einsum('bnhij,bnjhd->bnihd', p, vb)
    return o.reshape(B, S, ATT_W)


def moe_ffn(x, w_router, b_router, w_gate, b_gate, w_up, b_up, w_down, b_down):
    T, D = x.shape
    logits = (x @ w_router + b_router).astype(jnp.float32)
    top_val, top_idx = lax.top_k(logits, TOP_K)
    top_w = jax.nn.softmax(top_val, axis=-1)
    e_flat = top_idx.reshape(-1)
    tok_flat = jnp.repeat(jnp.arange(T, dtype=jnp.int32), TOP_K)
    w_flat = top_w.reshape(-1)
    order = jnp.argsort(e_flat)
    e_sorted = e_flat[order]
    counts = jnp.bincount(e_flat, length=N_EXPERTS)
    starts = jnp.cumsum(counts) - counts
    padded = (counts + MOE_BLOCK - 1) // MOE_BLOCK * MOE_BLOCK
    pends = jnp.cumsum(padded)
    pstarts = pends - padded
    dest = pstarts[e_sorted] + (jnp.arange(T * TOP_K) - starts[e_sorted])
    n_rows = -(-(T * TOP_K + N_EXPERTS * (MOE_BLOCK - 1)) // MOE_BLOCK) * MOE_BLOCK
    n_blocks = n_rows // MOE_BLOCK
    row_tok = jnp.full((n_rows,), T, jnp.int32).at[dest].set(tok_flat[order])
    row_w = jnp.zeros((n_rows,), jnp.float32).at[dest].set(w_flat[order])
    block_start = jnp.arange(n_blocks) * MOE_BLOCK
    block_e = jnp.minimum(jnp.searchsorted(pends, block_start, side='right'), N_EXPERTS - 1)
    x_pad = jnp.concatenate([x, jnp.zeros((1, D), x.dtype)], axis=0)

    def run_block(args):
        tok, wts, e = args
        xb = x_pad[tok]
        g = jnp.minimum(xb @ w_gate[e] + b_gate[e], SWIGLU_LIMIT)
        u = jnp.clip(xb @ w_up[e] + b_up[e], -SWIGLU_LIMIT, SWIGLU_LIMIT)
        hdn = (u + 1.0) * (g * jax.nn.sigmoid(SWIGLU_ALPHA * g))
        return ((hdn @ w_down[e] + b_down[e]) * wts[:, None]).astype(x.dtype)

    out = lax.map(run_block, (row_tok.reshape(n_blocks, MOE_BLOCK),
                              row_w.reshape(n_blocks, MOE_BLOCK), block_e))
    y = jnp.zeros((T + 1, D), x.dtype).at[row_tok].add(out.reshape(n_rows, D))
    return y[:T]


def setup_inputs(seed: int = 0) -> dict:
    key = jax.random.key(seed)
    ks = jax.random.split(key, 24)
    f32 = jnp.float32
    L = DEPTH

    def nrm(k, shape, scale):
        return jax.random.normal(k, shape, f32) * scale

    return {
        'x': nrm(ks[0], (BATCH, SEQ, D_MODEL), 1.0),
        'norm1_g': 1.0 + nrm(ks[1], (L, D_MODEL), 0.02),
        'w_in': nrm(ks[2], (L, D_MODEL, IN_COLS), D_MODEL ** -0.5),
        'w_gla_gate_up': nrm(ks[3], (L, GLA_GATE_RANK, GLA_QK), GLA_GATE_RANK ** -0.5),
        'b_gla_gate': nrm(ks[4], (L, GLA_QK), 0.1),
        'gla_norm_g': 1.0 + nrm(ks[5], (L, GLA_DV), 0.02),
        'q_norm_g': 1.0 + nrm(ks[6], (L, ATT_HD), 0.02),
        'k_norm_g': 1.0 + nrm(ks[7], (L, ATT_HD), 0.02),
        'rel_bias': nrm(ks[8], (L, ATT_HEADS, 2 * REL_CLIP + 1), 0.2),
        'b_branch_gate': nrm(ks[9], (L, 2, D_MODEL), 0.02),
        'w_branch_gla': nrm(ks[10], (L, GLA_V, D_MODEL), GLA_V ** -0.5),
        'w_branch_att': nrm(ks[11], (L, ATT_W, D_MODEL), ATT_W ** -0.5),
        'w_out': nrm(ks[12], (L, D_MODEL, D_MODEL), D_MODEL ** -0.5),
        'norm2_g': 1.0 + nrm(ks[13], (L, D_MODEL), 0.02),
        'w_router': nrm(ks[14], (L, D_MODEL, N_EXPERTS), D_MODEL ** -0.5),
        'b_router': nrm(ks[15], (L, N_EXPERTS), 0.01),
        'w_gate': nrm(ks[16], (L, N_EXPERTS, D_MODEL, D_FF), D_MODEL ** -0.5),
        'b_gate': nrm(ks[17], (L, N_EXPERTS, D_FF), 0.02),
        'w_up': nrm(ks[18], (L, N_EXPERTS, D_MODEL, D_FF), D_MODEL ** -0.5),
        'b_up': nrm(ks[19], (L, N_EXPERTS, D_FF), 0.02),
        'w_down': nrm(ks[20], (L, N_EXPERTS, D_FF, D_MODEL), D_FF ** -0.5),
        'b_down': nrm(ks[21], (L, N_EXPERTS, D_MODEL), 0.02),
    }


def reference(x, norm1_g, w_in, w_gla_gate_up, b_gla_gate, gla_norm_g, q_norm_g, k_norm_g,
              rel_bias, b_branch_gate, w_branch_gla, w_branch_att, w_out, norm2_g,
              w_router, b_router, w_gate, b_gate, w_up, b_up, w_down, b_down):
    B, S, D = x.shape
    split_points = [int(c) for c in np.cumsum(SPLITS)[:-1]]
    h = x
    for l in range(DEPTH):
        xn = rms_norm(h, norm1_g[l])
        proj = xn @ w_in[l]
        (g_q, g_k, g_v, g_r, g_lr, a_q, a_k, a_v,
         gate_gla, gate_att) = jnp.split(proj, split_points, axis=-1)
        y_gla = gla_mixer(g_q, g_k, g_v, g_r, g_lr, w_gla_gate_up[l], b_gla_gate[l],
                          gla_norm_g[l])
        y_att = chunk_attention(a_q, a_k, a_v, q_norm_g[l], k_norm_g[l], rel_bias[l])
        s_gla = jax.nn.sigmoid(gate_gla + b_branch_gate[l, 0])
        s_att = jax.nn.sigmoid(gate_att + b_branch_gate[l, 1])
        merged = s_gla * (y_gla @ w_branch_gla[l]) + s_att * (y_att @ w_branch_att[l])
        h = h + merged @ w_out[l]
        hn = rms_norm(h, norm2_g[l]).reshape(B * S, D)
        h = h + moe_ffn(hn, w_router[l], b_router[l], w_gate[l], b_gate[l], w_up[l],
                        b_up[l], w_down[l], b_down[l]).reshape(B, S, D)
    return h
```

```python
import functools

import jax
import jax.numpy as jnp
import numpy as np
from jax import lax
from jax.experimental import pallas as pl
from jax.experimental.pallas import tpu as pltpu

F32 = jnp.float32
BF16 = jnp.bfloat16
I32 = jnp.int32

D_MODEL = 2048
CHUNK = 64
RMS_EPS = 1e-6
GLA_HEADS = 4
GLA_DK = 256
GLA_DV = 512
GLA_RANK = 16
GLA_TAU = 16.0
GLA_QK = GLA_HEADS * GLA_DK
GLA_V = GLA_HEADS * GLA_DV
ATT_HEADS = 8
ATT_HD = 128
ATT_W = ATT_HEADS * ATT_HD
ATT_LEFT = 8
REL_CLIP = 128
N_EXPERTS = 32
TOP_K = 4
D_FF = 2048
SWIGLU_LIMIT = 7.0
SWIGLU_ALPHA = 1.702

COLS_A = 2 * GLA_QK + 2 * GLA_V
COL_LR = COLS_A
COL_B = COLS_A + GLA_RANK
COLS_B = 3 * ATT_W + 2 * D_MODEL

VMEM_LIMIT = 56 * 1024 * 1024
NEG_BIG = -1e30

ROW_BLK = 256
SEG_BLKS = 8
SEG_ROWS = ROW_BLK * SEG_BLKS
FF_TILE = 256


def _cparams(sem):
    return pltpu.CompilerParams(dimension_semantics=sem, vmem_limit_bytes=VMEM_LIMIT)


def _norm1_kernel(x_ref, g_ref, wlr_ref, xn_ref, glr_ref):
    x = x_ref[...]
    ms = jnp.mean(x * x, axis=-1, keepdims=True)
    y = (x * lax.rsqrt(ms + RMS_EPS) * g_ref[...]).astype(BF16)
    xn_ref[...] = y
    glr_ref[...] = jnp.dot(y, wlr_ref[...], preferred_element_type=F32)


def _norm1(x2, g, wlr):
    T = x2.shape[0]
    tm = 512
    return pl.pallas_call(
        _norm1_kernel,
        grid=(T // tm,),
        in_specs=[pl.BlockSpec((tm, D_MODEL), lambda i: (i, 0)),
                  pl.BlockSpec((1, D_MODEL), lambda i: (0, 0)),
                  pl.BlockSpec((D_MODEL, GLA_RANK), lambda i: (0, 0))],
        out_specs=[pl.BlockSpec((tm, D_MODEL), lambda i: (i, 0)),
                   pl.BlockSpec((tm, GLA_RANK), lambda i: (i, 0))],
        out_shape=[jax.ShapeDtypeStruct((T, D_MODEL), BF16),
                   jax.ShapeDtypeStruct((T, GLA_RANK), F32)],
        compiler_params=_cparams(("parallel",)),
        name="norm1",
    )(x2, g, wlr)


def _proj_kernel(x_ref, w_ref, o_ref, wb_ref):
    @pl.when(pl.program_id(1) == 0)
    def _():
        wb_ref[...] = w_ref[...].astype(BF16)

    o_ref[...] = jnp.dot(x_ref[...], wb_ref[...], preferred_element_type=F32).astype(o_ref.dtype)


def _proj(xn, w, n_cols, name):
    T, K = xn.shape
    tm, tn = 1024, 1024
    return pl.pallas_call(
        _proj_kernel,
        grid=(n_cols // tn, T // tm),
        in_specs=[pl.BlockSpec((tm, K), lambda j, i: (i, 0)),
                  pl.BlockSpec((K, tn), lambda j, i: (0, j))],
        out_specs=pl.BlockSpec((tm, tn), lambda j, i: (i, j)),
        out_shape=jax.ShapeDtypeStruct((T, n_cols), BF16),
        scratch_shapes=[pltpu.VMEM((K, tn), BF16)],
        compiler_params=_cparams(("arbitrary", "arbitrary")),
        name=name,
    )(xn, w)


GLA_ROWS = 256


def _gla_kernel(q_ref, k_ref, v_ref, r_ref, glr_ref, wup_ref, bg_ref, ng_ref, o_ref, st_ref):
    @pl.when(pl.program_id(1) == 0)
    def _():
        st_ref[...] = jnp.zeros_like(st_ref)

    z = jnp.dot(glr_ref[...], wup_ref[...], preferred_element_type=F32,
                precision=lax.Precision.HIGHEST) + bg_ref[...]
    log_a = (jnp.minimum(z, 0.0) - jnp.log(1.0 + jnp.exp(-jnp.abs(z)))) * (1.0 / GLA_TAU)

    ri = lax.broadcasted_iota(I32, (CHUNK, CHUNK), 0)
    ci = lax.broadcasted_iota(I32, (CHUNK, CHUNK), 1)
    causal = ci <= ri
    tril = jnp.where(causal, 1.0, 0.0).astype(BF16)
    scale = GLA_DK ** -0.5
    ng = ng_ref[...]

    for c in range(GLA_ROWS // CHUNK):
        rows = slice(c * CHUNK, (c + 1) * CHUNK)
        la = log_a[rows]
        la_hi = la.astype(BF16)
        la_lo = (la - la_hi.astype(F32)).astype(BF16)
        bcum = (jnp.dot(tril, la_hi, preferred_element_type=F32)
                + jnp.dot(tril, la_lo, preferred_element_type=F32))
        blast = bcum[CHUNK - 1:CHUNK, :]
        qc = q_ref[rows, :].astype(F32) * scale
        kc = k_ref[rows, :].astype(F32)
        q_dec = (qc * jnp.exp(bcum)).astype(BF16)
        k_inv = (kc * jnp.exp(-bcum)).astype(BF16)
        k_end = (kc * jnp.exp(blast - bcum)).astype(BF16)
        vv = v_ref[rows, :]
        a = lax.dot_general(q_dec, k_inv, (((1,), (1,)), ((), ())), preferred_element_type=F32)
        a = jnp.where(causal, a, 0.0).astype(BF16)
        st = st_ref[...]
        o = jnp.dot(a, vv, preferred_element_type=F32)
        o = o + lax.dot_general(q_dec, st.astype(BF16), (((1,), (1,)), ((), ())),
                                preferred_element_type=F32)
        st_ref[...] = st * jnp.exp(blast) + lax.dot_general(
            vv, k_end, (((0,), (0,)), ((), ())), preferred_element_type=F32)
        ms = jnp.mean(o * o, axis=-1, keepdims=True)
        y = o * lax.rsqrt(ms + RMS_EPS) * ng
        rr = r_ref[rows, :].astype(F32)
        o_ref[rows, :] = (y * (rr * jax.nn.sigmoid(rr))).astype(o_ref.dtype)


def _gla(proj_a, glr, wup, bg, ng):
    T = proj_a.shape[0]
    R = GLA_ROWS
    kq, kv = GLA_QK // GLA_DK, GLA_QK // GLA_DV
    return pl.pallas_call(
        _gla_kernel,
        grid=(GLA_HEADS, T // R),
        in_specs=[pl.BlockSpec((R, GLA_DK), lambda h, n: (n, h)),
                  pl.BlockSpec((R, GLA_DK), lambda h, n: (n, kq + h)),
                  pl.BlockSpec((R, GLA_DV), lambda h, n: (n, 2 * kv + h)),
                  pl.BlockSpec((R, GLA_DV), lambda h, n: (n, 2 * kv + GLA_HEADS + h)),
                  pl.BlockSpec((R, GLA_RANK), lambda h, n: (n, 0)),
                  pl.BlockSpec((GLA_RANK, GLA_DK), lambda h, n: (0, h)),
                  pl.BlockSpec((1, GLA_DK), lambda h, n: (0, h)),
                  pl.BlockSpec((1, GLA_DV), lambda h, n: (0, 0))],
        out_specs=pl.BlockSpec((R, GLA_DV), lambda h, n: (n, h)),
        out_shape=jax.ShapeDtypeStruct((T, GLA_V), BF16),
        scratch_shapes=[pltpu.VMEM((GLA_DV, GLA_DK), F32)],
        compiler_params=_cparams(("arbitrary", "arbitrary")),
        name="gla",
    )(proj_a, proj_a, proj_a, proj_a, glr, wup, bg, ng)


ATT_QROWS = 256
ATT_PAD = ATT_LEFT * CHUNK
ATT_BAND = ATT_PAD + ATT_QROWS


def _att_bias_table(rel_bias):
    i = np.arange(ATT_QROWS)[:, None]
    j = np.arange(ATT_BAND)[None, :]
    rel = np.clip(i - j + ATT_PAD, -REL_CLIP, REL_CLIP) + REL_CLIP
    qc, kc = i // CHUNK, j // CHUNK
    valid = (kc >= qc) & (kc <= qc + ATT_LEFT)
    tab = rel_bias.astype(F32)[:, rel]
    return jnp.where(valid[None], tab, NEG_BIG)


def _att_kernel(q_ref, k_ref, v_ref, qg_ref, kg_ref, tab_ref, o_ref, kn_ref, vp_ref):
    qb = pl.program_id(1)

    @pl.when(qb == 0)
    def _():
        kk = k_ref[...].astype(F32)
        ms = jnp.mean(kk * kk, axis=-1, keepdims=True)
        kn = kk * lax.rsqrt(ms + RMS_EPS) * kg_ref[...]
        kn_ref[0:ATT_PAD, :] = jnp.zeros((ATT_PAD, ATT_HD), BF16)
        kn_ref[ATT_PAD:, :] = kn.astype(BF16)
        vp_ref[0:ATT_PAD, :] = jnp.zeros((ATT_PAD, ATT_HD), BF16)
        vp_ref[ATT_PAD:, :] = v_ref[...]

    q = q_ref[...].astype(F32)
    ms = jnp.mean(q * q, axis=-1, keepdims=True)
    qn = (q * lax.rsqrt(ms + RMS_EPS) * qg_ref[...] * (ATT_HD ** -0.5)).astype(BF16)
    start = pl.multiple_of(qb * ATT_QROWS, ATT_QROWS)
    kb = kn_ref[pl.ds(start, ATT_BAND), :]
    vb = vp_ref[pl.ds(start, ATT_BAND), :]
    s = lax.dot_general(qn, kb, (((1,), (1,)), ((), ())), preferred_element_type=F32)
    s = s + tab_ref[...]
    kpos = start - ATT_PAD + lax.broadcasted_iota(I32, (1, ATT_BAND), 1)
    s = jnp.where(kpos >= 0, s, NEG_BIG)
    m = jnp.max(s, axis=-1, keepdims=True)
    p = jnp.exp(s - m)
    l = jnp.sum(p, axis=-1, keepdims=True)
    o = jnp.dot(p.astype(BF16), vb, preferred_element_type=F32)
    o_ref[...] = (o / l).astype(o_ref.dtype)


def _attention(proj_b, qg, kg, tab):
    T = proj_b.shape[0]
    R = ATT_QROWS
    return pl.pallas_call(
        _att_kernel,
        grid=(ATT_HEADS, T // R),
        in_specs=[pl.BlockSpec((R, ATT_HD), lambda h, n: (n, h)),
                  pl.BlockSpec((T, ATT_HD), lambda h, n: (0, ATT_HEADS + h)),
                  pl.BlockSpec((T, ATT_HD), lambda h, n: (0, 2 * ATT_HEADS + h)),
                  pl.BlockSpec((1, ATT_HD), lambda h, n: (0, 0)),
                  pl.BlockSpec((1, ATT_HD), lambda h, n: (0, 0)),
                  pl.BlockSpec((None, R, ATT_BAND), lambda h, n: (h, 0, 0))],
        out_specs=pl.BlockSpec((R, ATT_HD), lambda h, n: (n, h)),
        out_shape=jax.ShapeDtypeStruct((T, ATT_W), BF16),
        scratch_shapes=[pltpu.VMEM((T + ATT_PAD, ATT_HD), BF16),
                        pltpu.VMEM((T + ATT_PAD, ATT_HD), BF16)],
        compiler_params=_cparams(("arbitrary", "arbitrary")),
        name="chunk_attention",
    )(proj_b, proj_b, proj_b, qg, kg, tab)


def _merge_kernel(yg_ref, ya_ref, wg_ref, wa_ref, gg_ref, ga_ref, bb_ref, o_ref):
    pg = jnp.dot(yg_ref[...], wg_ref[...], preferred_element_type=F32)
    pa = jnp.dot(ya_ref[...], wa_ref[...], preferred_element_type=F32)
    bb = bb_ref[...]
    sg = jax.nn.sigmoid(gg_ref[...].astype(F32) + bb[0:1, :])
    sa = jax.nn.sigmoid(ga_ref[...].astype(F32) + bb[1:2, :])
    o_ref[...] = (sg * pg + sa * pa).astype(o_ref.dtype)


def _merge(y_gla, y_att, wg, wa, proj_b, bb):
    T = y_gla.shape[0]
    tm, tn = 1024, 1024
    gate0 = 3 * ATT_W // tn
    return pl.pallas_call(
        _merge_kernel,
        grid=(D_MODEL // tn, T // tm),
        in_specs=[pl.BlockSpec((tm, GLA_V), lambda j, i: (i, 0)),
                  pl.BlockSpec((tm, ATT_W), lambda j, i: (i, 0)),
                  pl.BlockSpec((GLA_V, tn), lambda j, i: (0, j)),
                  pl.BlockSpec((ATT_W, tn), lambda j, i: (0, j)),
                  pl.BlockSpec((tm, tn), lambda j, i: (i, gate0 + j)),
                  pl.BlockSpec((tm, tn), lambda j, i: (i, gate0 + D_MODEL // tn + j)),
                  pl.BlockSpec((2, tn), lambda j, i: (0, j))],
        out_specs=pl.BlockSpec((tm, tn), lambda j, i: (i, j)),
        out_shape=jax.ShapeDtypeStruct((T, D_MODEL), BF16),
        compiler_params=_cparams(("arbitrary", "arbitrary")),
        name="merge",
    )(y_gla, y_att, wg, wa, proj_b, proj_b, bb)


ROUTE_ROWS = 256
ROUTE_LANES = 128


def _route_kernel(m_ref, x_ref, wo_ref, g2_ref, wr_ref, br_ref,
                  h1_ref, hn_ref, idx_ref, wt_ref, rank_ref, cnt_ref, run_ref):
    @pl.when(pl.program_id(0) == 0)
    def _():
        run_ref[...] = jnp.zeros_like(run_ref)

    h1 = x_ref[...] + jnp.dot(m_ref[...], wo_ref[...], preferred_element_type=F32)
    h1_ref[...] = h1
    ms = jnp.mean(h1 * h1, axis=-1, keepdims=True)
    hn = h1 * lax.rsqrt(ms + RMS_EPS) * g2_ref[...]
    hn_ref[...] = hn
    logits = jnp.dot(hn, wr_ref[...], preferred_element_type=F32,
                     precision=lax.Precision.HIGHEST) + br_ref[...]

    R = ROUTE_ROWS
    lanes = lax.broadcasted_iota(I32, (R, N_EXPERTS), 1)
    work = logits
    vals, sels, idxs = [], [], []
    for _ in range(TOP_K):
        m = jnp.max(work, axis=-1, keepdims=True)
        idx = jnp.min(jnp.where(work == m, lanes, N_EXPERTS), axis=-1, keepdims=True)
        sel = lanes == idx
        vals.append(m)
        idxs.append(idx)
        sels.append(sel)
        work = jnp.where(sel, -jnp.inf, work)
    es = [jnp.exp(v - vals[0]) for v in vals]
    denom = es[0] + es[1] + es[2] + es[3]

    hot = jnp.zeros((R, N_EXPERTS), F32)
    for sel in sels:
        hot = jnp.where(sel, 1.0, hot)
    ri = lax.broadcasted_iota(I32, (R, R), 0)
    ci = lax.broadcasted_iota(I32, (R, R), 1)
    strict = jnp.where(ci < ri, 1.0, 0.0).astype(BF16)
    before = jnp.dot(strict, hot.astype(BF16), preferred_element_type=F32) + run_ref[...]
    run_new = run_ref[...] + jnp.sum(hot, axis=0, keepdims=True)
    run_ref[...] = run_new
    cnt_ref[...] = run_new

    ol = lax.broadcasted_iota(I32, (R, ROUTE_LANES), 1)
    idx_o = jnp.zeros((R, ROUTE_LANES), I32)
    wt_o = jnp.zeros((R, ROUTE_LANES), F32)
    rank_o = jnp.zeros((R, ROUTE_LANES), I32)
    for k in range(TOP_K):
        rk = jnp.sum(jnp.where(sels[k], before, 0.0), axis=-1, keepdims=True).astype(I32)
        idx_o = jnp.where(ol == k, idxs[k], idx_o)
        wt_o = jnp.where(ol == k, es[k] / denom, wt_o)
        rank_o = jnp.where(ol == k, rk, rank_o)
    idx_ref[...] = idx_o
    wt_ref[...] = wt_o
    rank_ref[...] = rank_o


def _route(merged, x2, wo, g2, wr, br):
    T = x2.shape[0]
    R = ROUTE_ROWS
    row = lambda i: (i, 0)
    fixed = lambda i: (0, 0)
    return pl.pallas_call(
        _route_kernel,
        grid=(T // R,),
        in_specs=[pl.BlockSpec((R, D_MODEL), row),
                  pl.BlockSpec((R, D_MODEL), row),
                  pl.BlockSpec((D_MODEL, D_MODEL), fixed),
                  pl.BlockSpec((1, D_MODEL), fixed),
                  pl.BlockSpec((D_MODEL, N_EXPERTS), fixed),
                  pl.BlockSpec((1, N_EXPERTS), fixed)],
        out_specs=[pl.BlockSpec((R, D_MODEL), row),
                   pl.BlockSpec((R, D_MODEL), row),
                   pl.BlockSpec((R, ROUTE_LANES), row),
                   pl.BlockSpec((R, ROUTE_LANES), row),
                   pl.BlockSpec((R, ROUTE_LANES), row),
                   pl.BlockSpec((1, N_EXPERTS), fixed)],
        out_shape=[jax.ShapeDtypeStruct((T, D_MODEL), F32),
                   jax.ShapeDtypeStruct((T, D_MODEL), F32),
                   jax.ShapeDtypeStruct((T, ROUTE_LANES), I32),
                   jax.ShapeDtypeStruct((T, ROUTE_LANES), F32),
                   jax.ShapeDtypeStruct((T, ROUTE_LANES), I32),
                   jax.ShapeDtypeStruct((1, N_EXPERTS), F32)],
        scratch_shapes=[pltpu.VMEM((1, N_EXPERTS), F32)],
        compiler_params=_cparams(("arbitrary",)),
        name="outproj_route",
    )(merged, x2, wo, g2, wr, br)


DISP_ROWS = 256


def _dispatch_kernel(pos_ref, hn_ref, xs_ref, sem):
    def row_copy(t, k):
        dst = pos_ref[0, 0, t * TOP_K + k]
        return pltpu.make_async_copy(hn_ref.at[pl.ds(t, 1), :], xs_ref.at[pl.ds(dst, 1), :], sem)

    def issue(t, carry):
        for k in range(TOP_K):
            row_copy(t, k).start()
        return carry

    lax.fori_loop(0, DISP_ROWS, issue, 0)

    def drain(t, carry):
        for k in range(TOP_K):
            row_copy(t, k).wait()
        return carry

    lax.fori_loop(0, DISP_ROWS, drain, 0)


def _dispatch(hn, pos, n_alloc):
    T = hn.shape[0]
    R = DISP_ROWS
    pos3 = pos.reshape(T // R, 1, R * TOP_K)
    return pl.pallas_call(
        _dispatch_kernel,
        grid=(T // R,),
        in_specs=[pl.BlockSpec((1, 1, R * TOP_K), lambda i: (i, 0, 0), memory_space=pltpu.SMEM),
                  pl.BlockSpec((R, D_MODEL), lambda i: (i, 0))],
        out_specs=pl.BlockSpec(memory_space=pl.ANY),
        out_shape=jax.ShapeDtypeStruct((n_alloc, D_MODEL), F32),
        scratch_shapes=[pltpu.SemaphoreType.DMA(())],
        compiler_params=_cparams(("arbitrary",)),
        name="dispatch",
    )(pos3, hn)


def _expert_kernel(ue_ref, ur_ref, un_ref, xs_ref, wg_ref, wu_ref, wd_ref, bg_ref, bu_ref, bd_ref,
                   ys_ref, xb_ref, acc_ref, stage_ref, sem):
    u = pl.program_id(0)
    j = pl.program_id(1)
    nblk = un_ref[u]
    row0 = ur_ref[u]

    def blk_rows(b):
        return pl.ds(pl.multiple_of(b * ROW_BLK, ROW_BLK), ROW_BLK)

    def hbm_rows(b):
        return pl.ds(pl.multiple_of(row0 + b * ROW_BLK, ROW_BLK), ROW_BLK)

    @pl.when(j == 0)
    def _():
        def load(b, carry):
            cp = pltpu.make_async_copy(xs_ref.at[hbm_rows(b), :], stage_ref, sem)
            cp.start()
            cp.wait()
            xb_ref[blk_rows(b), :] = stage_ref[...].astype(BF16)
            return carry

        lax.fori_loop(0, nblk, load, 0)

    wg = wg_ref[...].astype(BF16)
    wu = wu_ref[...].astype(BF16)
    wd = wd_ref[...].astype(BF16)
    bg = bg_ref[...]
    bu = bu_ref[...]

    def body(b, carry):
        xb = xb_ref[blk_rows(b), :]
        g = jnp.minimum(jnp.dot(xb, wg, preferred_element_type=F32) + bg, SWIGLU_LIMIT)
        up = jnp.clip(jnp.dot(xb, wu, preferred_element_type=F32) + bu, -SWIGLU_LIMIT, SWIGLU_LIMIT)
        hdn = ((up + 1.0) * (g * jax.nn.sigmoid(SWIGLU_ALPHA * g))).astype(BF16)
        y = jnp.dot(hdn, wd, preferred_element_type=F32)

        @pl.when(j == 0)
        def _():
            acc_ref[blk_rows(b), :] = y

        @pl.when(j > 0)
        def _():
            acc_ref[blk_rows(b), :] += y

        return carry

    lax.fori_loop(0, nblk, body, 0)

    @pl.when(j == pl.num_programs(1) - 1)
    def _():
        bd = bd_ref[...]

        def store(b, carry):
            stage_ref[...] = acc_ref[blk_rows(b), :] + bd
            cp = pltpu.make_async_copy(stage_ref, ys_ref.at[hbm_rows(b), :], sem)
            cp.start()
            cp.wait()
            return carry

        lax.fori_loop(0, nblk, store, 0)


def _experts(xs, unit_e, unit_row0, unit_nblk, w_gate, w_up, w_down, b_gate, b_up, b_down):
    n_alloc = xs.shape[0]
    n_units = unit_e.shape[0]
    nj = D_FF // FF_TILE
    last = nj - 1

    def jj(j, un, u):
        return jnp.where(un[u] > 0, j, last)

    grid_spec = pltpu.PrefetchScalarGridSpec(
        num_scalar_prefetch=3,
        grid=(n_units, nj),
        in_specs=[
            pl.BlockSpec(memory_space=pl.ANY),
            pl.BlockSpec((None, D_MODEL, FF_TILE), lambda u, j, ue, ur, un: (ue[u], 0, jj(j, un, u))),
            pl.BlockSpec((None, D_MODEL, FF_TILE), lambda u, j, ue, ur, un: (ue[u], 0, jj(j, un, u))),
            pl.BlockSpec((None, FF_TILE, D_MODEL), lambda u, j, ue, ur, un: (ue[u], jj(j, un, u), 0)),
            pl.BlockSpec((None, 1, FF_TILE), lambda u, j, ue, ur, un: (ue[u], 0, jj(j, un, u))),
            pl.BlockSpec((None, 1, FF_TILE), lambda u, j, ue, ur, un: (ue[u], 0, jj(j, un, u))),
            pl.BlockSpec((None, 1, D_MODEL), lambda u, j, ue, ur, un: (ue[u], 0, 0)),
        ],
        out_specs=pl.BlockSpec(memory_space=pl.ANY),
        scratch_shapes=[pltpu.VMEM((SEG_ROWS, D_MODEL), BF16),
                        pltpu.VMEM((SEG_ROWS, D_MODEL), F32),
                        pltpu.VMEM((ROW_BLK, D_MODEL), F32),
                        pltpu.SemaphoreType.DMA(())],
    )
    return pl.pallas_call(
        _expert_kernel,
        grid_spec=grid_spec,
        out_shape=jax.ShapeDtypeStruct((n_alloc, D_MODEL), F32),
        compiler_params=_cparams(("arbitrary", "arbitrary")),
        name="experts",
    )(unit_e, unit_row0, unit_nblk, xs, w_gate, w_up, w_down,
      b_gate.reshape(N_EXPERTS, 1, D_FF), b_up.reshape(N_EXPERTS, 1, D_FF),
      b_down.reshape(N_EXPERTS, 1, D_MODEL))


COMB_ROWS = 128


def _combine_kernel(pos_ref, h1_ref, wt_ref, ys_ref, o_ref, buf_ref, sem):
    def row_copy(t, k):
        src = pos_ref[0, 0, t * TOP_K + k]
        return pltpu.make_async_copy(ys_ref.at[pl.ds(src, 1), :],
                                     buf_ref.at[k, pl.ds(t, 1), :], sem)

    def issue(t, carry):
        for k in range(TOP_K):
            row_copy(t, k).start()
        return carry

    lax.fori_loop(0, COMB_ROWS, issue, 0)

    def drain(t, carry):
        for k in range(TOP_K):
            row_copy(t, k).wait()
        return carry

    lax.fori_loop(0, COMB_ROWS, drain, 0)

    wt = wt_ref[...]
    acc = h1_ref[...]
    for k in range(TOP_K):
        acc = acc + wt[:, k:k + 1] * buf_ref[k]
    o_ref[...] = acc


def _combine(h1, wt, ys, pos):
    T = h1.shape[0]
    R = COMB_ROWS
    pos3 = pos.reshape(T // R, 1, R * TOP_K)
    return pl.pallas_call(
        _combine_kernel,
        grid=(T // R,),
        in_specs=[pl.BlockSpec((1, 1, R * TOP_K), lambda i: (i, 0, 0), memory_space=pltpu.SMEM),
                  pl.BlockSpec((R, D_MODEL), lambda i: (i, 0)),
                  pl.BlockSpec((R, ROUTE_LANES), lambda i: (i, 0)),
                  pl.BlockSpec(memory_space=pl.ANY)],
        out_specs=pl.BlockSpec((R, D_MODEL), lambda i: (i, 0)),
        out_shape=jax.ShapeDtypeStruct((T, D_MODEL), F32),
        scratch_shapes=[pltpu.VMEM((TOP_K, R, D_MODEL), F32),
                        pltpu.SemaphoreType.DMA(())],
        compiler_params=_cparams(("arbitrary",)),
        name="combine",
    )(pos3, h1, wt, ys)


def _max_units(T):
    max_blocks = (T * TOP_K) // ROW_BLK + N_EXPERTS
    return max_blocks // SEG_BLKS + N_EXPERTS


def _routing_tables(counts, idx, rank, T):
    counts = counts.astype(I32)
    nblk = (counts + ROW_BLK - 1) // ROW_BLK
    pstart = (jnp.cumsum(nblk) - nblk) * ROW_BLK
    pos = jnp.take(pstart, idx, axis=0) + rank
    units = (nblk + SEG_BLKS - 1) // SEG_BLKS
    uend = jnp.cumsum(units)
    ustart = uend - units
    n_units = _max_units(T)
    u = jnp.arange(n_units, dtype=I32)
    total = uend[-1]
    ue = jnp.minimum(jnp.searchsorted(uend, u, side="right").astype(I32), N_EXPERTS - 1)
    local = u - jnp.take(ustart, ue)
    un = jnp.clip(jnp.take(nblk, ue) - local * SEG_BLKS, 0, SEG_BLKS)
    ur = jnp.take(pstart, ue) + local * SEG_ROWS
    live = u < total
    last_e = jnp.take(ue, jnp.maximum(total - 1, 0))
    ue = jnp.where(live, ue, last_e)
    un = jnp.where(live, un, 0)
    ur = jnp.where(live, ur, 0)
    return pos.astype(I32), ue, ur.astype(I32), un.astype(I32)


def kernel(x, norm1_g, w_in, w_gla_gate_up, b_gla_gate, gla_norm_g, q_norm_g, k_norm_g, rel_bias,
           b_branch_gate, w_branch_gla, w_branch_att, w_out, norm2_g, w_router, b_router,
           w_gate, b_gate, w_up, b_up, w_down, b_down):
    B, S, D = x.shape
    T = B * S
    h = x.reshape(T, D)
    for l in range(w_in.shape[0]):
        wl = w_in[l]
        xn, glr = _norm1(h, norm1_g[l].reshape(1, D), wl[:, COL_LR:COL_B].astype(BF16))
        proj_a = _proj(xn, wl, COLS_A, "inproj_gla")
        proj_b = _proj(xn, wl[:, COL_B:].astype(BF16), COLS_B, "inproj_att")
        y_gla = _gla(proj_a, glr, w_gla_gate_up[l], b_gla_gate[l].reshape(1, GLA_QK),
                     gla_norm_g[l].reshape(1, GLA_DV))
        y_att = _attention(proj_b, q_norm_g[l].reshape(1, ATT_HD), k_norm_g[l].reshape(1, ATT_HD),
                           _att_bias_table(rel_bias[l]))
        merged = _merge(y_gla, y_att, w_branch_gla[l].astype(BF16), w_branch_att[l].astype(BF16),
                        proj_b, b_branch_gate[l])
        h1, hn, idx, wt, rank, counts = _route(
            merged, h, w_out[l].astype(BF16), norm2_g[l].reshape(1, D), w_router[l],
            b_router[l].reshape(1, N_EXPERTS))
        pos, ue, ur, un = _routing_tables(counts[0], idx[:, :TOP_K], rank[:, :TOP_K], T)
        n_alloc = ((T * TOP_K) // ROW_BLK + N_EXPERTS) * ROW_BLK
        xs = _dispatch(hn, pos.reshape(-1), n_alloc)
        ys = _experts(xs, ue, ur, un, w_gate[l], w_up[l], w_down[l], b_gate[l], b_up[l], b_down[l])
        h = _combine(h1, wt, ys, pos.reshape(-1))
    return h.reshape(B, S, D)
```

```python
import functools

import jax
import jax.numpy as jnp
import numpy as np
from jax import lax
from jax.experimental import pallas as pl
from jax.experimental.pallas import tpu as pltpu

F32 = jnp.float32
BF16 = jnp.bfloat16
I32 = jnp.int32

D_MODEL = 2048
CHUNK = 64
RMS_EPS = 1e-6
GLA_HEADS = 4
GLA_DK = 256
GLA_DV = 512
GLA_RANK = 16
GLA_TAU = 16.0
GLA_QK = GLA_HEADS * GLA_DK
GLA_V = GLA_HEADS * GLA_DV
ATT_HEADS = 8
ATT_HD = 128
ATT_W = ATT_HEADS * ATT_HD
ATT_LEFT = 8
REL_CLIP = 128
N_EXPERTS = 32
TOP_K = 4
D_FF = 2048
SWIGLU_LIMIT = 7.0
SWIGLU_ALPHA = 1.702

COLS_A = 2 * GLA_QK + 2 * GLA_V
COL_LR = COLS_A
COL_B = COLS_A + GLA_RANK
COLS_B = 3 * ATT_W + 2 * D_MODEL

VMEM_LIMIT = 56 * 1024 * 1024
NEG_BIG = -1e30

ROW_BLK = 256
SEG_BLKS = 8
SEG_ROWS = ROW_BLK * SEG_BLKS
FF_TILE = 256
NJ = D_FF // FF_TILE
assert D_MODEL // FF_TILE == NJ and SEG_BLKS <= NJ


def _cparams(sem):
    return pltpu.CompilerParams(dimension_semantics=sem, vmem_limit_bytes=VMEM_LIMIT)


def _norm1_kernel(x_ref, g_ref, wlr_ref, xn_ref, glr_ref):
    x = x_ref[...]
    ms = jnp.mean(x * x, axis=-1, keepdims=True)
    y = (x * lax.rsqrt(ms + RMS_EPS) * g_ref[...]).astype(BF16)
    xn_ref[...] = y
    glr_ref[...] = jnp.dot(y, wlr_ref[...], preferred_element_type=F32)


def _norm1(x2, g, wlr):
    T = x2.shape[0]
    tm = 512
    return pl.pallas_call(
        _norm1_kernel,
        grid=(T // tm,),
        in_specs=[pl.BlockSpec((tm, D_MODEL), lambda i: (i, 0)),
                  pl.BlockSpec((1, D_MODEL), lambda i: (0, 0)),
                  pl.BlockSpec((D_MODEL, GLA_RANK), lambda i: (0, 0))],
        out_specs=[pl.BlockSpec((tm, D_MODEL), lambda i: (i, 0)),
                   pl.BlockSpec((tm, GLA_RANK), lambda i: (i, 0))],
        out_shape=[jax.ShapeDtypeStruct((T, D_MODEL), BF16),
                   jax.ShapeDtypeStruct((T, GLA_RANK), F32)],
        compiler_params=_cparams(("parallel",)),
        name="norm1",
    )(x2, g, wlr)


def _proj_kernel(x_ref, w_ref, o_ref, wb_ref):
    @pl.when(pl.program_id(1) == 0)
    def _():
        wb_ref[...] = w_ref[...].astype(BF16)

    o_ref[...] = jnp.dot(x_ref[...], wb_ref[...], preferred_element_type=F32).astype(o_ref.dtype)


def _proj(xn, w, n_cols, name):
    T, K = xn.shape
    tm, tn = 1024, 1024
    return pl.pallas_call(
        _proj_kernel,
        grid=(n_cols // tn, T // tm),
        in_specs=[pl.BlockSpec((tm, K), lambda j, i: (i, 0)),
                  pl.BlockSpec((K, tn), lambda j, i: (0, j))],
        out_specs=pl.BlockSpec((tm, tn), lambda j, i: (i, j)),
        out_shape=jax.ShapeDtypeStruct((T, n_cols), BF16),
        scratch_shapes=[pltpu.VMEM((K, tn), BF16)],
        compiler_params=_cparams(("arbitrary", "arbitrary")),
        name=name,
    )(xn, w)


GLA_ROWS = 256


def _gla_kernel(q_ref, k_ref, v_ref, r_ref, glr_ref, wup_ref, bg_ref, ng_ref, o_ref, st_ref):
    @pl.when(pl.program_id(1) == 0)
    def _():
        st_ref[...] = jnp.zeros_like(st_ref)

    z = jnp.dot(glr_ref[...], wup_ref[...], preferred_element_type=F32,
                precision=lax.Precision.HIGHEST) + bg_ref[...]
    log_a = (jnp.minimum(z, 0.0) - jnp.log(1.0 + jnp.exp(-jnp.abs(z)))) * (1.0 / GLA_TAU)

    ri = lax.broadcasted_iota(I32, (CHUNK, CHUNK), 0)
    ci = lax.broadcasted_iota(I32, (CHUNK, CHUNK), 1)
    causal = ci <= ri
    tril = jnp.where(causal, 1.0, 0.0).astype(BF16)
    scale = GLA_DK ** -0.5
    ng = ng_ref[...]

    for c in range(GLA_ROWS // CHUNK):
        rows = slice(c * CHUNK, (c + 1) * CHUNK)
        la = log_a[rows]
        la_hi = la.astype(BF16)
        la_lo = (la - la_hi.astype(F32)).astype(BF16)
        bcum = (jnp.dot(tril, la_hi, preferred_element_type=F32)
                + jnp.dot(tril, la_lo, preferred_element_type=F32))
        blast = bcum[CHUNK - 1:CHUNK, :]
        qc = q_ref[rows, :].astype(F32) * scale
        kc = k_ref[rows, :].astype(F32)
        q_dec = (qc * jnp.exp(bcum)).astype(BF16)
        k_inv = (kc * jnp.exp(-bcum)).astype(BF16)
        k_end = (kc * jnp.exp(blast - bcum)).astype(BF16)
        vv = v_ref[rows, :]
        a = lax.dot_general(q_dec, k_inv, (((1,), (1,)), ((), ())), preferred_element_type=F32)
        a = jnp.where(causal, a, 0.0).astype(BF16)
        st = st_ref[...]
        o = jnp.dot(a, vv, preferred_element_type=F32)
        o = o + lax.dot_general(q_dec, st.astype(BF16), (((1,), (1,)), ((), ())),
                                preferred_element_type=F32)
        st_ref[...] = st * jnp.exp(blast) + lax.dot_general(
            vv, k_end, (((0,), (0,)), ((), ())), preferred_element_type=F32)
        ms = jnp.mean(o * o, axis=-1, keepdims=True)
        y = o * lax.rsqrt(ms + RMS_EPS) * ng
        rr = r_ref[rows, :].astype(F32)
        o_ref[rows, :] = (y * (rr * jax.nn.sigmoid(rr))).astype(o_ref.dtype)


def _gla(proj_a, glr, wup, bg, ng):
    T = proj_a.shape[0]
    R = GLA_ROWS
    kq, kv = GLA_QK // GLA_DK, GLA_QK // GLA_DV
    return pl.pallas_call(
        _gla_kernel,
        grid=(GLA_HEADS, T // R),
        in_specs=[pl.BlockSpec((R, GLA_DK), lambda h, n: (n, h)),
                  pl.BlockSpec((R, GLA_DK), lambda h, n: (n, kq + h)),
                  pl.BlockSpec((R, GLA_DV), lambda h, n: (n, 2 * kv + h)),
                  pl.BlockSpec((R, GLA_DV), lambda h, n: (n, 2 * kv + GLA_HEADS + h)),
                  pl.BlockSpec((R, GLA_RANK), lambda h, n: (n, 0)),
                  pl.BlockSpec((GLA_RANK, GLA_DK), lambda h, n: (0, h)),
                  pl.BlockSpec((1, GLA_DK), lambda h, n: (0, h)),
                  pl.BlockSpec((1, GLA_DV), lambda h, n: (0, 0))],
        out_specs=pl.BlockSpec((R, GLA_DV), lambda h, n: (n, h)),
        out_shape=jax.ShapeDtypeStruct((T, GLA_V), BF16),
        scratch_shapes=[pltpu.VMEM((GLA_DV, GLA_DK), F32)],
        compiler_params=_cparams(("arbitrary", "arbitrary")),
        name="gla",
    )(proj_a, proj_a, proj_a, proj_a, glr, wup, bg, ng)


ATT_QROWS = 256
ATT_PAD = ATT_LEFT * CHUNK
ATT_BAND = ATT_PAD + ATT_QROWS


ATT_EXT = 1024


def _att_bias_diagonals(rel_bias):
    m = np.arange(ATT_EXT)
    m = np.where(m < ATT_BAND, m, m - ATT_EXT)
    rel = np.clip(ATT_PAD - m, -REL_CLIP, REL_CLIP) + REL_CLIP
    return rel_bias.astype(F32)[:, rel].reshape(ATT_HEADS, 1, ATT_EXT)


def _att_kernel(q_ref, k_ref, v_ref, qg_ref, kg_ref, ext_ref, o_ref, kn_ref, vp_ref, tab_ref):
    qb = pl.program_id(1)

    @pl.when(qb == 0)
    def _():
        ext = jnp.broadcast_to(ext_ref[...], (ATT_QROWS, ATT_EXT))
        tab = pltpu.roll(ext, 0, 1, stride=1, stride_axis=0)[:, :ATT_BAND]
        qc = lax.broadcasted_iota(I32, (ATT_QROWS, ATT_BAND), 0) // CHUNK
        kc = lax.broadcasted_iota(I32, (ATT_QROWS, ATT_BAND), 1) // CHUNK
        tab_ref[...] = jnp.where((kc >= qc) & (kc <= qc + ATT_LEFT), tab, NEG_BIG)
        kk = k_ref[...].astype(F32)
        ms = jnp.mean(kk * kk, axis=-1, keepdims=True)
        kn = kk * lax.rsqrt(ms + RMS_EPS) * kg_ref[...]
        kn_ref[0:ATT_PAD, :] = jnp.zeros((ATT_PAD, ATT_HD), BF16)
        kn_ref[ATT_PAD:, :] = kn.astype(BF16)
        vp_ref[0:ATT_PAD, :] = jnp.zeros((ATT_PAD, ATT_HD), BF16)
        vp_ref[ATT_PAD:, :] = v_ref[...]

    q = q_ref[...].astype(F32)
    ms = jnp.mean(q * q, axis=-1, keepdims=True)
    qn = (q * lax.rsqrt(ms + RMS_EPS) * qg_ref[...] * (ATT_HD ** -0.5)).astype(BF16)
    start = pl.multiple_of(qb * ATT_QROWS, ATT_QROWS)
    kb = kn_ref[pl.ds(start, ATT_BAND), :]
    vb = vp_ref[pl.ds(start, ATT_BAND), :]
    s = lax.dot_general(qn, kb, (((1,), (1,)), ((), ())), preferred_element_type=F32)
    s = s + tab_ref[...]
    kpos = start - ATT_PAD + lax.broadcasted_iota(I32, (1, ATT_BAND), 1)
    s = jnp.where(kpos >= 0, s, NEG_BIG)
    m = jnp.max(s, axis=-1, keepdims=True)
    p = jnp.exp(s - m)
    l = jnp.sum(p, axis=-1, keepdims=True)
    o = jnp.dot(p.astype(BF16), vb, preferred_element_type=F32)
    o_ref[...] = (o / l).astype(o_ref.dtype)


def _attention(proj_b, qg, kg, ext):
    T = proj_b.shape[0]
    R = ATT_QROWS
    return pl.pallas_call(
        _att_kernel,
        grid=(ATT_HEADS, T // R),
        in_specs=[pl.BlockSpec((R, ATT_HD), lambda h, n: (n, h)),
                  pl.BlockSpec((T, ATT_HD), lambda h, n: (0, ATT_HEADS + h)),
                  pl.BlockSpec((T, ATT_HD), lambda h, n: (0, 2 * ATT_HEADS + h)),
                  pl.BlockSpec((1, ATT_HD), lambda h, n: (0, 0)),
                  pl.BlockSpec((1, ATT_HD), lambda h, n: (0, 0)),
                  pl.BlockSpec((None, 1, ATT_EXT), lambda h, n: (h, 0, 0))],
        out_specs=pl.BlockSpec((R, ATT_HD), lambda h, n: (n, h)),
        out_shape=jax.ShapeDtypeStruct((T, ATT_W), BF16),
        scratch_shapes=[pltpu.VMEM((T + ATT_PAD, ATT_HD), BF16),
                        pltpu.VMEM((T + ATT_PAD, ATT_HD), BF16),
                        pltpu.VMEM((R, ATT_BAND), F32)],
        compiler_params=_cparams(("arbitrary", "arbitrary")),
        name="chunk_attention",
    )(proj_b, proj_b, proj_b, qg, kg, ext)


def _merge_kernel(yg_ref, ya_ref, wg_ref, wa_ref, gg_ref, ga_ref, bb_ref, o_ref):
    pg = jnp.dot(yg_ref[...], wg_ref[...], preferred_element_type=F32)
    pa = jnp.dot(ya_ref[...], wa_ref[...], preferred_element_type=F32)
    bb = bb_ref[...]
    sg = jax.nn.sigmoid(gg_ref[...].astype(F32) + bb[0:1, :])
    sa = jax.nn.sigmoid(ga_ref[...].astype(F32) + bb[1:2, :])
    o_ref[...] = (sg * pg + sa * pa).astype(o_ref.dtype)


def _merge(y_gla, y_att, wg, wa, proj_b, bb):
    T = y_gla.shape[0]
    tm, tn = 1024, 1024
    gate0 = 3 * ATT_W // tn
    return pl.pallas_call(
        _merge_kernel,
        grid=(D_MODEL // tn, T // tm),
        in_specs=[pl.BlockSpec((tm, GLA_V), lambda j, i: (i, 0)),
                  pl.BlockSpec((tm, ATT_W), lambda j, i: (i, 0)),
                  pl.BlockSpec((GLA_V, tn), lambda j, i: (0, j)),
                  pl.BlockSpec((ATT_W, tn), lambda j, i: (0, j)),
                  pl.BlockSpec((tm, tn), lambda j, i: (i, gate0 + j)),
                  pl.BlockSpec((tm, tn), lambda j, i: (i, gate0 + D_MODEL // tn + j)),
                  pl.BlockSpec((2, tn), lambda j, i: (0, j))],
        out_specs=pl.BlockSpec((tm, tn), lambda j, i: (i, j)),
        out_shape=jax.ShapeDtypeStruct((T, D_MODEL), BF16),
        compiler_params=_cparams(("arbitrary", "arbitrary")),
        name="merge",
    )(y_gla, y_att, wg, wa, proj_b, proj_b, bb)


ROUTE_ROWS = 256
ROUTE_LANES = 128


def _route_kernel(m_ref, x_ref, wo_ref, g2_ref, wr_ref, br_ref,
                  h1_ref, hn_ref, idx_ref, wt_ref, rank_ref, cnt_ref, run_ref):
    @pl.when(pl.program_id(0) == 0)
    def _():
        run_ref[...] = jnp.zeros_like(run_ref)

    h1 = x_ref[...] + jnp.dot(m_ref[...], wo_ref[...], preferred_element_type=F32)
    h1_ref[...] = h1
    ms = jnp.mean(h1 * h1, axis=-1, keepdims=True)
    hn = h1 * lax.rsqrt(ms + RMS_EPS) * g2_ref[...]
    hn_ref[...] = hn
    logits = jnp.dot(hn, wr_ref[...], preferred_element_type=F32,
                     precision=lax.Precision.HIGHEST) + br_ref[...]

    R = ROUTE_ROWS
    lanes = lax.broadcasted_iota(I32, (R, N_EXPERTS), 1)
    work = logits
    vals, sels, idxs = [], [], []
    for _ in range(TOP_K):
        m = jnp.max(work, axis=-1, keepdims=True)
        idx = jnp.min(jnp.where(work == m, lanes, N_EXPERTS), axis=-1, keepdims=True)
        sel = lanes == idx
        vals.append(m)
        idxs.append(idx)
        sels.append(sel)
        work = jnp.where(sel, -jnp.inf, work)
    es = [jnp.exp(v - vals[0]) for v in vals]
    denom = es[0] + es[1] + es[2] + es[3]

    hot = jnp.zeros((R, N_EXPERTS), F32)
    for sel in sels:
        hot = jnp.where(sel, 1.0, hot)
    ri = lax.broadcasted_iota(I32, (R, R), 0)
    ci = lax.broadcasted_iota(I32, (R, R), 1)
    strict = jnp.where(ci < ri, 1.0, 0.0).astype(BF16)
    before = jnp.dot(strict, hot.astype(BF16), preferred_element_type=F32) + run_ref[...]
    run_new = run_ref[...] + jnp.sum(hot, axis=0, keepdims=True)
    run_ref[...] = run_new
    cnt_ref[...] = run_new

    ol = lax.broadcasted_iota(I32, (R, ROUTE_LANES), 1)
    idx_o = jnp.zeros((R, ROUTE_LANES), I32)
    wt_o = jnp.zeros((R, ROUTE_LANES), F32)
    rank_o = jnp.zeros((R, ROUTE_LANES), I32)
    for k in range(TOP_K):
        rk = jnp.sum(jnp.where(sels[k], before, 0.0), axis=-1, keepdims=True).astype(I32)
        idx_o = jnp.where(ol == k, idxs[k], idx_o)
        wt_o = jnp.where(ol == k, es[k] / denom, wt_o)
        rank_o = jnp.where(ol == k, rk, rank_o)
    idx_ref[...] = idx_o
    wt_ref[...] = wt_o
    rank_ref[...] = rank_o


def _route(merged, x2, wo, g2, wr, br):
    T = x2.shape[0]
    R = ROUTE_ROWS
    row = lambda i: (i, 0)
    fixed = lambda i: (0, 0)
    return pl.pallas_call(
        _route_kernel,
        grid=(T // R,),
        in_specs=[pl.BlockSpec((R, D_MODEL), row),
                  pl.BlockSpec((R, D_MODEL), row),
                  pl.BlockSpec((D_MODEL, D_MODEL), fixed),
                  pl.BlockSpec((1, D_MODEL), fixed),
                  pl.BlockSpec((D_MODEL, N_EXPERTS), fixed),
                  pl.BlockSpec((1, N_EXPERTS), fixed)],
        out_specs=[pl.BlockSpec((R, D_MODEL), row),
                   pl.BlockSpec((R, D_MODEL), row),
                   pl.BlockSpec((R, ROUTE_LANES), row),
                   pl.BlockSpec((R, ROUTE_LANES), row),
                   pl.BlockSpec((R, ROUTE_LANES), row),
                   pl.BlockSpec((1, N_EXPERTS), fixed)],
        out_shape=[jax.ShapeDtypeStruct((T, D_MODEL), F32),
                   jax.ShapeDtypeStruct((T, D_MODEL), F32),
                   jax.ShapeDtypeStruct((T, ROUTE_LANES), I32),
                   jax.ShapeDtypeStruct((T, ROUTE_LANES), F32),
                   jax.ShapeDtypeStruct((T, ROUTE_LANES), I32),
                   jax.ShapeDtypeStruct((1, N_EXPERTS), F32)],
        scratch_shapes=[pltpu.VMEM((1, N_EXPERTS), F32)],
        compiler_params=_cparams(("arbitrary",)),
        name="outproj_route",
    )(merged, x2, wo, g2, wr, br)


DISP_ROWS = 256


def _dispatch_kernel(pos_ref, hn_ref, xs_ref, sem):
    def row_copy(t, k):
        dst = pos_ref[0, 0, t * TOP_K + k]
        return pltpu.make_async_copy(hn_ref.at[pl.ds(t, 1), :], xs_ref.at[pl.ds(dst, 1), :], sem)

    def issue(t, carry):
        for k in range(TOP_K):
            row_copy(t, k).start()
        return carry

    lax.fori_loop(0, DISP_ROWS, issue, 0)

    def drain(t, carry):
        for k in range(TOP_K):
            row_copy(t, k).wait()
        return carry

    lax.fori_loop(0, DISP_ROWS, drain, 0)


def _dispatch(hn, pos, n_alloc):
    T = hn.shape[0]
    R = DISP_ROWS
    pos3 = pos.reshape(T // R, 1, R * TOP_K)
    return pl.pallas_call(
        _dispatch_kernel,
        grid=(T // R,),
        in_specs=[pl.BlockSpec((1, 1, R * TOP_K), lambda i: (i, 0, 0), memory_space=pltpu.SMEM),
                  pl.BlockSpec((R, D_MODEL), lambda i: (i, 0))],
        out_specs=pl.BlockSpec(memory_space=pl.ANY),
        out_shape=jax.ShapeDtypeStruct((n_alloc, D_MODEL), F32),
        scratch_shapes=[pltpu.SemaphoreType.DMA(())],
        compiler_params=_cparams(("arbitrary",)),
        name="dispatch",
    )(pos3, hn)


def _expert_kernel(ue_ref, ur_ref, un_ref, xs_ref, wg_ref, wu_ref, wd_ref, bg_ref, bu_ref, bd_ref,
                   ys_ref, xb_ref, h_ref, stage_ref, ybuf_ref, pend_ref, xsem, ysem):
    u = pl.program_id(0)
    s = pl.program_id(1)
    nu = pl.num_programs(0)
    nblk = un_ref[u]
    row0 = ur_ref[u]
    slot = u % 2

    def blk_rows(b):
        return pl.ds(pl.multiple_of(b * ROW_BLK, ROW_BLK), ROW_BLK)

    def x_copy(unit, b):
        rows = pl.ds(pl.multiple_of(ur_ref[unit] + b * ROW_BLK, ROW_BLK), ROW_BLK)
        return pltpu.make_async_copy(xs_ref.at[rows, :], stage_ref, xsem)

    def y_copy(b, n, ys):
        rows = pl.ds(pl.multiple_of(row0 + b * ROW_BLK, ROW_BLK), ROW_BLK)
        cols = pl.ds(pl.multiple_of(n * FF_TILE, FF_TILE), FF_TILE)
        return pltpu.make_async_copy(ybuf_ref.at[ys, blk_rows(b), :], ys_ref.at[rows, cols], ysem)

    def drain_stores():
        def wait_one(i, carry):
            y_copy(0, 0, 0).wait()
            return carry

        lax.fori_loop(0, pend_ref[0], wait_one, 0)
        pend_ref[0] = 0

    @pl.when((u == 0) & (s == 0))
    def _():
        pend_ref[0] = 0

        def load(b, carry):
            cp = x_copy(0, b)
            cp.start()
            cp.wait()
            xb_ref[0, blk_rows(b), :] = stage_ref[...].astype(BF16)
            return carry

        lax.fori_loop(0, nblk, load, 0)

    nxt = jnp.minimum(u + 1, nu - 1)
    prefetch = (u + 1 < nu) & (s < un_ref[nxt])

    @pl.when(prefetch)
    def _():
        x_copy(nxt, s).start()

    @pl.when(s < NJ)
    def _():
        wg = wg_ref[...].astype(BF16)
        wu = wu_ref[...].astype(BF16)
        bg = bg_ref[...]
        bu = bu_ref[...]

        def body(b, carry):
            xb = xb_ref[slot, blk_rows(b), :]
            g = jnp.minimum(jnp.dot(xb, wg, preferred_element_type=F32) + bg, SWIGLU_LIMIT)
            up = jnp.clip(jnp.dot(xb, wu, preferred_element_type=F32) + bu,
                          -SWIGLU_LIMIT, SWIGLU_LIMIT)
            h_ref[s, blk_rows(b), :] = ((up + 1.0) * (g * jax.nn.sigmoid(SWIGLU_ALPHA * g))).astype(BF16)
            return carry

        lax.fori_loop(0, nblk, body, 0)

    @pl.when(s >= NJ)
    def _():
        n = s - NJ
        ys = n % 2
        wd = wd_ref[...].astype(BF16)
        bd = bd_ref[...]

        def body(b, carry):
            hb = jnp.concatenate([h_ref[t, blk_rows(b), :] for t in range(NJ)], axis=1)
            ybuf_ref[ys, blk_rows(b), :] = jnp.dot(hb, wd, preferred_element_type=F32) + bd
            return carry

        lax.fori_loop(0, nblk, body, 0)
        drain_stores()

        def issue(b, carry):
            y_copy(b, n, ys).start()
            return carry

        lax.fori_loop(0, nblk, issue, 0)
        pend_ref[0] = nblk

    @pl.when(prefetch)
    def _():
        x_copy(nxt, s).wait()
        xb_ref[1 - slot, blk_rows(s), :] = stage_ref[...].astype(BF16)

    @pl.when((u == nu - 1) & (s == 2 * NJ - 1))
    def _():
        drain_stores()


def _experts(xs, unit_e, unit_row0, unit_nblk, w_gate, w_up, w_down, b_gate, b_up, b_down):
    n_alloc = xs.shape[0]
    n_units = unit_e.shape[0]
    last = NJ - 1

    def up_tile(u, s, ue, ur, un):
        return (ue[u], 0, jnp.where(un[u] > 0, jnp.minimum(s, last), last))

    def down_tile(u, s, ue, ur, un):
        return (ue[u], 0, jnp.where(un[u] > 0, jnp.maximum(s - NJ, 0), last))

    grid_spec = pltpu.PrefetchScalarGridSpec(
        num_scalar_prefetch=3,
        grid=(n_units, 2 * NJ),
        in_specs=[
            pl.BlockSpec(memory_space=pl.ANY),
            pl.BlockSpec((None, D_MODEL, FF_TILE), up_tile),
            pl.BlockSpec((None, D_MODEL, FF_TILE), up_tile),
            pl.BlockSpec((None, D_FF, FF_TILE), down_tile),
            pl.BlockSpec((None, 1, FF_TILE), up_tile),
            pl.BlockSpec((None, 1, FF_TILE), up_tile),
            pl.BlockSpec((None, 1, FF_TILE), down_tile),
        ],
        out_specs=pl.BlockSpec(memory_space=pl.ANY),
        scratch_shapes=[pltpu.VMEM((2, SEG_ROWS, D_MODEL), BF16),
                        pltpu.VMEM((NJ, SEG_ROWS, FF_TILE), BF16),
                        pltpu.VMEM((ROW_BLK, D_MODEL), F32),
                        pltpu.VMEM((2, SEG_ROWS, FF_TILE), F32),
                        pltpu.SMEM((1,), I32),
                        pltpu.SemaphoreType.DMA(()),
                        pltpu.SemaphoreType.DMA(())],
    )
    return pl.pallas_call(
        _expert_kernel,
        grid_spec=grid_spec,
        out_shape=jax.ShapeDtypeStruct((n_alloc, D_MODEL), F32),
        compiler_params=_cparams(("arbitrary", "arbitrary")),
        name="experts",
    )(unit_e, unit_row0, unit_nblk, xs, w_gate, w_up, w_down,
      b_gate.reshape(N_EXPERTS, 1, D_FF), b_up.reshape(N_EXPERTS, 1, D_FF),
      b_down.reshape(N_EXPERTS, 1, D_MODEL))


COMB_ROWS = 128


def _combine_kernel(pos_ref, h1_ref, wt_ref, ys_ref, o_ref, buf_ref, sem):
    def row_copy(t, k):
        src = pos_ref[0, 0, t * TOP_K + k]
        return pltpu.make_async_copy(ys_ref.at[pl.ds(src, 1), :],
                                     buf_ref.at[k, pl.ds(t, 1), :], sem)

    def issue(t, carry):
        for k in range(TOP_K):
            row_copy(t, k).start()
        return carry

    lax.fori_loop(0, COMB_ROWS, issue, 0)

    def drain(t, carry):
        for k in range(TOP_K):
            row_copy(t, k).wait()
        return carry

    lax.fori_loop(0, COMB_ROWS, drain, 0)

    wt = wt_ref[...]
    acc = h1_ref[...]
    for k in range(TOP_K):
        acc = acc + wt[:, k:k + 1] * buf_ref[k]
    o_ref[...] = acc


def _combine(h1, wt, ys, pos):
    T = h1.shape[0]
    R = COMB_ROWS
    pos3 = pos.reshape(T // R, 1, R * TOP_K)
    return pl.pallas_call(
        _combine_kernel,
        grid=(T // R,),
        in_specs=[pl.BlockSpec((1, 1, R * TOP_K), lambda i: (i, 0, 0), memory_space=pltpu.SMEM),
                  pl.BlockSpec((R, D_MODEL), lambda i: (i, 0)),
                  pl.BlockSpec((R, ROUTE_LANES), lambda i: (i, 0)),
                  pl.BlockSpec(memory_space=pl.ANY)],
        out_specs=pl.BlockSpec((R, D_MODEL), lambda i: (i, 0)),
        out_shape=jax.ShapeDtypeStruct((T, D_MODEL), F32),
        scratch_shapes=[pltpu.VMEM((TOP_K, R, D_MODEL), F32),
                        pltpu.SemaphoreType.DMA(())],
        compiler_params=_cparams(("arbitrary",)),
        name="combine",
    )(pos3, h1, wt, ys)


def _max_units(T):
    max_blocks = (T * TOP_K) // ROW_BLK + N_EXPERTS
    return max_blocks // SEG_BLKS + N_EXPERTS


def _routing_tables(counts, idx, rank, T):
    counts = counts.astype(I32)
    nblk = (counts + ROW_BLK - 1) // ROW_BLK
    pstart = (jnp.cumsum(nblk) - nblk) * ROW_BLK
    pos = jnp.take(pstart, idx, axis=0) + rank
    units = (nblk + SEG_BLKS - 1) // SEG_BLKS
    uend = jnp.cumsum(units)
    ustart = uend - units
    n_units = _max_units(T)
    u = jnp.arange(n_units, dtype=I32)
    total = uend[-1]
    ue = jnp.minimum(jnp.searchsorted(uend, u, side="right").astype(I32), N_EXPERTS - 1)
    local = u - jnp.take(ustart, ue)
    un = jnp.clip(jnp.take(nblk, ue) - local * SEG_BLKS, 0, SEG_BLKS)
    ur = jnp.take(pstart, ue) + local * SEG_ROWS
    live = u < total
    last_e = jnp.take(ue, jnp.maximum(total - 1, 0))
    ue = jnp.where(live, ue, last_e)
    un = jnp.where(live, un, 0)
    ur = jnp.where(live, ur, 0)
    return pos.astype(I32), ue, ur.astype(I32), un.astype(I32)


def kernel(x, norm1_g, w_in, w_gla_gate_up, b_gla_gate, gla_norm_g, q_norm_g, k_norm_g, rel_bias,
           b_branch_gate, w_branch_gla, w_branch_att, w_out, norm2_g, w_router, b_router,
           w_gate, b_gate, w_up, b_up, w_down, b_down):
    B, S, D = x.shape
    T = B * S
    h = x.reshape(T, D)
    for l in range(w_in.shape[0]):
        wl = w_in[l]
        xn, glr = _norm1(h, norm1_g[l].reshape(1, D), wl[:, COL_LR:COL_B].astype(BF16))
        proj_a = _proj(xn, wl, COLS_A, "inproj_gla")
        proj_b = _proj(xn, wl[:, COL_B:].astype(BF16), COLS_B, "inproj_att")
        y_gla = _gla(proj_a, glr, w_gla_gate_up[l], b_gla_gate[l].reshape(1, GLA_QK),
                     gla_norm_g[l].reshape(1, GLA_DV))
        y_att = _attention(proj_b, q_norm_g[l].reshape(1, ATT_HD), k_norm_g[l].reshape(1, ATT_HD),
                           _att_bias_diagonals(rel_bias[l]))
        merged = _merge(y_gla, y_att, w_branch_gla[l].astype(BF16), w_branch_att[l].astype(BF16),
                        proj_b, b_branch_gate[l])
        h1, hn, idx, wt, rank, counts = _route(
            merged, h, w_out[l].astype(BF16), norm2_g[l].reshape(1, D), w_router[l],
            b_router[l].reshape(1, N_EXPERTS))
        pos, ue, ur, un = _routing_tables(counts[0], idx[:, :TOP_K], rank[:, :TOP_K], T)
        n_alloc = ((T * TOP_K) // ROW_BLK + N_EXPERTS) * ROW_BLK
        xs = _dispatch(hn, pos.reshape(-1), n_alloc)
        ys = _experts(xs, ue, ur, un, w_gate[l], w_up[l], w_down[l], b_gate[l], b_up[l], b_down[l])
        h = _combine(h1, wt, ys, pos.reshape(-1))
    return h.reshape(B, S, D)
```

```python
import functools

import jax
import jax.numpy as jnp
import numpy as np
from jax import lax
from jax.experimental import pallas as pl
from jax.experimental.pallas import tpu as pltpu

F32 = jnp.float32
BF16 = jnp.bfloat16
I32 = jnp.int32

D_MODEL = 2048
CHUNK = 64
RMS_EPS = 1e-6
GLA_HEADS = 4
GLA_DK = 256
GLA_DV = 512
GLA_RANK = 16
GLA_TAU = 16.0
GLA_QK = GLA_HEADS * GLA_DK
GLA_V = GLA_HEADS * GLA_DV
ATT_HEADS = 8
ATT_HD = 128
ATT_W = ATT_HEADS * ATT_HD
ATT_LEFT = 8
REL_CLIP = 128
N_EXPERTS = 32
TOP_K = 4
D_FF = 2048
SWIGLU_LIMIT = 7.0
SWIGLU_ALPHA = 1.702

COLS_A = 2 * GLA_QK + 2 * GLA_V
COL_LR = COLS_A
COL_B = COLS_A + GLA_RANK
COLS_B = 3 * ATT_W + 2 * D_MODEL

VMEM_LIMIT = 56 * 1024 * 1024
NEG_BIG = -1e30

ROW_BLK = 256
SEG_BLKS = 8
SEG_ROWS = ROW_BLK * SEG_BLKS
FF_TILE = 256
NJ = D_FF // FF_TILE
assert D_MODEL // FF_TILE == NJ and SEG_BLKS <= NJ


def _cparams(sem):
    return pltpu.CompilerParams(dimension_semantics=sem, vmem_limit_bytes=VMEM_LIMIT)


def _norm1_kernel(x_ref, g_ref, wlr_ref, xn_ref, glr_ref):
    x = x_ref[...]
    ms = jnp.mean(x * x, axis=-1, keepdims=True)
    y = (x * lax.rsqrt(ms + RMS_EPS) * g_ref[...]).astype(BF16)
    xn_ref[...] = y
    glr_ref[...] = jnp.dot(y, wlr_ref[...], preferred_element_type=F32)


def _norm1(x2, g, wlr):
    T = x2.shape[0]
    tm = 512
    return pl.pallas_call(
        _norm1_kernel,
        grid=(T // tm,),
        in_specs=[pl.BlockSpec((tm, D_MODEL), lambda i: (i, 0)),
                  pl.BlockSpec((1, D_MODEL), lambda i: (0, 0)),
                  pl.BlockSpec((D_MODEL, GLA_RANK), lambda i: (0, 0))],
        out_specs=[pl.BlockSpec((tm, D_MODEL), lambda i: (i, 0)),
                   pl.BlockSpec((tm, GLA_RANK), lambda i: (i, 0))],
        out_shape=[jax.ShapeDtypeStruct((T, D_MODEL), BF16),
                   jax.ShapeDtypeStruct((T, GLA_RANK), F32)],
        compiler_params=_cparams(("parallel",)),
        name="norm1",
    )(x2, g, wlr)


def _proj_kernel(x_ref, w_ref, o_ref, wb_ref):
    @pl.when(pl.program_id(1) == 0)
    def _():
        wb_ref[...] = w_ref[...].astype(BF16)

    o_ref[...] = jnp.dot(x_ref[...], wb_ref[...], preferred_element_type=F32).astype(o_ref.dtype)


def _proj(xn, w, layer, n_cols, name):
    T, K = xn.shape
    tm, tn = 1024, 1024
    return pl.pallas_call(
        _proj_kernel,
        grid=(n_cols // tn, T // tm),
        in_specs=[pl.BlockSpec((tm, K), lambda j, i: (i, 0)),
                  pl.BlockSpec((None, K, tn), lambda j, i: (layer, 0, j))],
        out_specs=pl.BlockSpec((tm, tn), lambda j, i: (i, j)),
        out_shape=jax.ShapeDtypeStruct((T, n_cols), BF16),
        scratch_shapes=[pltpu.VMEM((K, tn), BF16)],
        compiler_params=_cparams(("arbitrary", "arbitrary")),
        name=name,
    )(xn, w)


GLA_ROWS = 256


def _gla_kernel(q_ref, k_ref, v_ref, r_ref, glr_ref, wup_ref, bg_ref, ng_ref, o_ref, st_ref):
    @pl.when(pl.program_id(0) == 0)
    def _():
        st_ref[...] = jnp.zeros_like(st_ref)

    z = jnp.dot(glr_ref[...], wup_ref[...], preferred_element_type=F32,
                precision=lax.Precision.HIGHEST) + bg_ref[...]
    log_a = (jnp.minimum(z, 0.0) - jnp.log(1.0 + jnp.exp(-jnp.abs(z)))) * (1.0 / GLA_TAU)

    ri = lax.broadcasted_iota(I32, (CHUNK, CHUNK), 0)
    ci = lax.broadcasted_iota(I32, (CHUNK, CHUNK), 1)
    causal = ci <= ri
    tril = jnp.where(causal, 1.0, 0.0).astype(BF16)
    scale = GLA_DK ** -0.5
    ng = ng_ref[...]

    for c in range(GLA_ROWS // CHUNK):
        rows = slice(c * CHUNK, (c + 1) * CHUNK)
        la = log_a[rows]
        la_hi = la.astype(BF16)
        la_lo = (la - la_hi.astype(F32)).astype(BF16)
        bcum_all = (jnp.dot(tril, la_hi, preferred_element_type=F32)
                    + jnp.dot(tril, la_lo, preferred_element_type=F32))
        for h in range(GLA_HEADS):
            kcols = slice(h * GLA_DK, (h + 1) * GLA_DK)
            vcols = slice(h * GLA_DV, (h + 1) * GLA_DV)
            bcum = bcum_all[:, kcols]
            blast = bcum[CHUNK - 1:CHUNK, :]
            qc = q_ref[rows, kcols].astype(F32) * scale
            kc = k_ref[rows, kcols].astype(F32)
            q_dec = (qc * jnp.exp(bcum)).astype(BF16)
            k_inv = (kc * jnp.exp(-bcum)).astype(BF16)
            k_end = (kc * jnp.exp(blast - bcum)).astype(BF16)
            vv = v_ref[rows, vcols]
            a = lax.dot_general(q_dec, k_inv, (((1,), (1,)), ((), ())), preferred_element_type=F32)
            a = jnp.where(causal, a, 0.0).astype(BF16)
            st = st_ref[h]
            o = jnp.dot(a, vv, preferred_element_type=F32)
            o = o + lax.dot_general(q_dec, st.astype(BF16), (((1,), (1,)), ((), ())),
                                    preferred_element_type=F32)
            st_ref[h] = st * jnp.exp(blast) + lax.dot_general(
                vv, k_end, (((0,), (0,)), ((), ())), preferred_element_type=F32)
            ms = jnp.mean(o * o, axis=-1, keepdims=True)
            y = o * lax.rsqrt(ms + RMS_EPS) * ng
            rr = r_ref[rows, vcols].astype(F32)
            o_ref[rows, vcols] = (y * (rr * jax.nn.sigmoid(rr))).astype(o_ref.dtype)


def _gla(proj_a, glr, wup, bg, ng):
    T = proj_a.shape[0]
    R = GLA_ROWS
    return pl.pallas_call(
        _gla_kernel,
        grid=(T // R,),
        in_specs=[pl.BlockSpec((R, GLA_QK), lambda n: (n, 0)),
                  pl.BlockSpec((R, GLA_QK), lambda n: (n, 1)),
                  pl.BlockSpec((R, GLA_V), lambda n: (n, 1)),
                  pl.BlockSpec((R, GLA_V), lambda n: (n, 2)),
                  pl.BlockSpec((R, GLA_RANK), lambda n: (n, 0)),
                  pl.BlockSpec((GLA_RANK, GLA_QK), lambda n: (0, 0)),
                  pl.BlockSpec((1, GLA_QK), lambda n: (0, 0)),
                  pl.BlockSpec((1, GLA_DV), lambda n: (0, 0))],
        out_specs=pl.BlockSpec((R, GLA_V), lambda n: (n, 0)),
        out_shape=jax.ShapeDtypeStruct((T, GLA_V), BF16),
        scratch_shapes=[pltpu.VMEM((GLA_HEADS, GLA_DV, GLA_DK), F32)],
        compiler_params=_cparams(("arbitrary",)),
        name="gla",
    )(proj_a, proj_a, proj_a, proj_a, glr, wup, bg, ng)


ATT_QROWS = 256
ATT_PAD = ATT_LEFT * CHUNK
ATT_BAND = ATT_PAD + ATT_QROWS


ATT_EXT = 1024


def _att_bias_diagonals(rel_bias):
    m = np.arange(ATT_EXT)
    m = np.where(m < ATT_BAND, m, m - ATT_EXT)
    rel = np.clip(ATT_PAD - m, -REL_CLIP, REL_CLIP) + REL_CLIP
    return rel_bias.astype(F32)[:, rel].reshape(ATT_HEADS, 1, ATT_EXT)


def _att_kernel(q_ref, k_ref, v_ref, qg_ref, kg_ref, ext_ref, o_ref, kn_ref, vp_ref, tab_ref):
    qb = pl.program_id(1)

    @pl.when(qb == 0)
    def _():
        ext = jnp.broadcast_to(ext_ref[...], (ATT_QROWS, ATT_EXT))
        tab = pltpu.roll(ext, 0, 1, stride=1, stride_axis=0)[:, :ATT_BAND]
        qc = lax.broadcasted_iota(I32, (ATT_QROWS, ATT_BAND), 0) // CHUNK
        kc = lax.broadcasted_iota(I32, (ATT_QROWS, ATT_BAND), 1) // CHUNK
        tab_ref[...] = jnp.where((kc >= qc) & (kc <= qc + ATT_LEFT), tab, NEG_BIG)
        kk = k_ref[...].astype(F32)
        ms = jnp.mean(kk * kk, axis=-1, keepdims=True)
        kn = kk * lax.rsqrt(ms + RMS_EPS) * kg_ref[...]
        kn_ref[0:ATT_PAD, :] = jnp.zeros((ATT_PAD, ATT_HD), BF16)
        kn_ref[ATT_PAD:, :] = kn.astype(BF16)
        vp_ref[0:ATT_PAD, :] = jnp.zeros((ATT_PAD, ATT_HD), BF16)
        vp_ref[ATT_PAD:, :] = v_ref[...]

    q = q_ref[...].astype(F32)
    ms = jnp.mean(q * q, axis=-1, keepdims=True)
    qn = (q * lax.rsqrt(ms + RMS_EPS) * qg_ref[...] * (ATT_HD ** -0.5)).astype(BF16)
    start = pl.multiple_of(qb * ATT_QROWS, ATT_QROWS)
    kb = kn_ref[pl.ds(start, ATT_BAND), :]
    vb = vp_ref[pl.ds(start, ATT_BAND), :]
    s = lax.dot_general(qn, kb, (((1,), (1,)), ((), ())), preferred_element_type=F32)
    s = s + tab_ref[...]
    kpos = start - ATT_PAD + lax.broadcasted_iota(I32, (1, ATT_BAND), 1)
    s = jnp.where(kpos >= 0, s, NEG_BIG)
    m = jnp.max(s, axis=-1, keepdims=True)
    p = jnp.exp(s - m)
    l = jnp.sum(p, axis=-1, keepdims=True)
    o = jnp.dot(p.astype(BF16), vb, preferred_element_type=F32)
    o_ref[...] = (o / l).astype(o_ref.dtype)


def _attention(proj_b, qg, kg, ext):
    T = proj_b.shape[0]
    R = ATT_QROWS
    return pl.pallas_call(
        _att_kernel,
        grid=(ATT_HEADS, T // R),
        in_specs=[pl.BlockSpec((R, ATT_HD), lambda h, n: (n, h)),
                  pl.BlockSpec((T, ATT_HD), lambda h, n: (0, ATT_HEADS + h)),
                  pl.BlockSpec((T, ATT_HD), lambda h, n: (0, 2 * ATT_HEADS + h)),
                  pl.BlockSpec((1, ATT_HD), lambda h, n: (0, 0)),
                  pl.BlockSpec((1, ATT_HD), lambda h, n: (0, 0)),
                  pl.BlockSpec((None, 1, ATT_EXT), lambda h, n: (h, 0, 0))],
        out_specs=pl.BlockSpec((R, ATT_HD), lambda h, n: (n, h)),
        out_shape=jax.ShapeDtypeStruct((T, ATT_W), BF16),
        scratch_shapes=[pltpu.VMEM((T + ATT_PAD, ATT_HD), BF16),
                        pltpu.VMEM((T + ATT_PAD, ATT_HD), BF16),
                        pltpu.VMEM((R, ATT_BAND), F32)],
        compiler_params=_cparams(("arbitrary", "arbitrary")),
        name="chunk_attention",
    )(proj_b, proj_b, proj_b, qg, kg, ext)


def _merge_kernel(yg_ref, ya_ref, wg_ref, wa_ref, gg_ref, ga_ref, bb_ref, o_ref):
    pg = jnp.dot(yg_ref[...], wg_ref[...], preferred_element_type=F32)
    pa = jnp.dot(ya_ref[...], wa_ref[...], preferred_element_type=F32)
    bb = bb_ref[...]
    sg = jax.nn.sigmoid(gg_ref[...].astype(F32) + bb[0:1, :])
    sa = jax.nn.sigmoid(ga_ref[...].astype(F32) + bb[1:2, :])
    o_ref[...] = (sg * pg + sa * pa).astype(o_ref.dtype)


def _merge(y_gla, y_att, wg, wa, proj_b, bb):
    T = y_gla.shape[0]
    tm, tn = 1024, 1024
    gate0 = 3 * ATT_W // tn
    return pl.pallas_call(
        _merge_kernel,
        grid=(D_MODEL // tn, T // tm),
        in_specs=[pl.BlockSpec((tm, GLA_V), lambda j, i: (i, 0)),
                  pl.BlockSpec((tm, ATT_W), lambda j, i: (i, 0)),
                  pl.BlockSpec((GLA_V, tn), lambda j, i: (0, j)),
                  pl.BlockSpec((ATT_W, tn), lambda j, i: (0, j)),
                  pl.BlockSpec((tm, tn), lambda j, i: (i, gate0 + j)),
                  pl.BlockSpec((tm, tn), lambda j, i: (i, gate0 + D_MODEL // tn + j)),
                  pl.BlockSpec((2, tn), lambda j, i: (0, j))],
        out_specs=pl.BlockSpec((tm, tn), lambda j, i: (i, j)),
        out_shape=jax.ShapeDtypeStruct((T, D_MODEL), BF16),
        compiler_params=_cparams(("arbitrary", "arbitrary")),
        name="merge",
    )(y_gla, y_att, wg, wa, proj_b, proj_b, bb)


ROUTE_ROWS = 256
ROUTE_LANES = 128


def _route_kernel(m_ref, x_ref, wo_ref, g2_ref, wr_ref, br_ref,
                  h1_ref, hn_ref, idx_ref, wt_ref, rank_ref, cnt_ref, run_ref):
    @pl.when(pl.program_id(0) == 0)
    def _():
        run_ref[...] = jnp.zeros_like(run_ref)

    h1 = x_ref[...] + jnp.dot(m_ref[...], wo_ref[...], preferred_element_type=F32)
    h1_ref[...] = h1
    ms = jnp.mean(h1 * h1, axis=-1, keepdims=True)
    hn = h1 * lax.rsqrt(ms + RMS_EPS) * g2_ref[...]
    hn_ref[...] = hn
    wr = wr_ref[...]
    hn_hi = hn.astype(BF16)
    hn_lo = (hn - hn_hi.astype(F32)).astype(BF16)
    wr_hi = wr.astype(BF16)
    wr_lo = (wr - wr_hi.astype(F32)).astype(BF16)
    logits = (jnp.dot(hn_hi, wr_hi, preferred_element_type=F32)
              + jnp.dot(hn_hi, wr_lo, preferred_element_type=F32)
              + jnp.dot(hn_lo, wr_hi, preferred_element_type=F32)) + br_ref[...]

    R = ROUTE_ROWS
    lanes = lax.broadcasted_iota(I32, (R, N_EXPERTS), 1)
    work = logits
    vals, sels, idxs = [], [], []
    for _ in range(TOP_K):
        m = jnp.max(work, axis=-1, keepdims=True)
        idx = jnp.min(jnp.where(work == m, lanes, N_EXPERTS), axis=-1, keepdims=True)
        sel = lanes == idx
        vals.append(m)
        idxs.append(idx)
        sels.append(sel)
        work = jnp.where(sel, -jnp.inf, work)
    es = [jnp.exp(v - vals[0]) for v in vals]
    denom = es[0] + es[1] + es[2] + es[3]

    hot = jnp.zeros((R, N_EXPERTS), F32)
    for sel in sels:
        hot = jnp.where(sel, 1.0, hot)
    ri = lax.broadcasted_iota(I32, (R, R), 0)
    ci = lax.broadcasted_iota(I32, (R, R), 1)
    strict = jnp.where(ci < ri, 1.0, 0.0).astype(BF16)
    before = jnp.dot(strict, hot.astype(BF16), preferred_element_type=F32) + run_ref[...]
    run_new = run_ref[...] + jnp.sum(hot, axis=0, keepdims=True)
    run_ref[...] = run_new
    cnt_ref[...] = run_new

    ol = lax.broadcasted_iota(I32, (R, ROUTE_LANES), 1)
    idx_o = jnp.zeros((R, ROUTE_LANES), I32)
    wt_o = jnp.zeros((R, ROUTE_LANES), F32)
    rank_o = jnp.zeros((R, ROUTE_LANES), I32)
    for k in range(TOP_K):
        rk = jnp.sum(jnp.where(sels[k], before, 0.0), axis=-1, keepdims=True).astype(I32)
        idx_o = jnp.where(ol == k, idxs[k], idx_o)
        wt_o = jnp.where(ol == k, es[k] / denom, wt_o)
        rank_o = jnp.where(ol == k, rk, rank_o)
    idx_ref[...] = idx_o
    wt_ref[...] = wt_o
    rank_ref[...] = rank_o


def _route(merged, x2, wo, g2, wr, br):
    T = x2.shape[0]
    R = ROUTE_ROWS
    row = lambda i: (i, 0)
    fixed = lambda i: (0, 0)
    return pl.pallas_call(
        _route_kernel,
        grid=(T // R,),
        in_specs=[pl.BlockSpec((R, D_MODEL), row),
                  pl.BlockSpec((R, D_MODEL), row),
                  pl.BlockSpec((D_MODEL, D_MODEL), fixed),
                  pl.BlockSpec((1, D_MODEL), fixed),
                  pl.BlockSpec((D_MODEL, N_EXPERTS), fixed),
                  pl.BlockSpec((1, N_EXPERTS), fixed)],
        out_specs=[pl.BlockSpec((R, D_MODEL), row),
                   pl.BlockSpec((R, D_MODEL), row),
                   pl.BlockSpec((R, ROUTE_LANES), row),
                   pl.BlockSpec((R, ROUTE_LANES), row),
                   pl.BlockSpec((R, ROUTE_LANES), row),
                   pl.BlockSpec((1, N_EXPERTS), fixed)],
        out_shape=[jax.ShapeDtypeStruct((T, D_MODEL), F32),
                   jax.ShapeDtypeStruct((T, D_MODEL), F32),
                   jax.ShapeDtypeStruct((T, ROUTE_LANES), I32),
                   jax.ShapeDtypeStruct((T, ROUTE_LANES), F32),
                   jax.ShapeDtypeStruct((T, ROUTE_LANES), I32),
                   jax.ShapeDtypeStruct((1, N_EXPERTS), F32)],
        scratch_shapes=[pltpu.VMEM((1, N_EXPERTS), F32)],
        compiler_params=_cparams(("arbitrary",)),
        name="outproj_route",
    )(merged, x2, wo, g2, wr, br)


DISP_ROWS = 512
DMA_UNROLL = 8


def _dispatch_kernel(pos_ref, hn_ref, xs_ref, sem):
    def row_copy(t, k):
        dst = pos_ref[0, 0, t * TOP_K + k]
        return pltpu.make_async_copy(hn_ref.at[pl.ds(t, 1), :], xs_ref.at[pl.ds(dst, 1), :], sem)

    def issue(i, carry):
        for tt in range(DMA_UNROLL):
            for k in range(TOP_K):
                row_copy(i * DMA_UNROLL + tt, k).start(priority=k % 2)
        return carry

    lax.fori_loop(0, DISP_ROWS // DMA_UNROLL, issue, 0)

    def drain(i, carry):
        for tt in range(DMA_UNROLL):
            for k in range(TOP_K):
                row_copy(i * DMA_UNROLL + tt, k).wait()
        return carry

    lax.fori_loop(0, DISP_ROWS // DMA_UNROLL, drain, 0)


def _dispatch(hn, pos, n_alloc):
    T = hn.shape[0]
    R = DISP_ROWS
    pos3 = pos.reshape(T // R, 1, R * TOP_K)
    return pl.pallas_call(
        _dispatch_kernel,
        grid=(T // R,),
        in_specs=[pl.BlockSpec((1, 1, R * TOP_K), lambda i: (i, 0, 0), memory_space=pltpu.SMEM),
                  pl.BlockSpec((R, D_MODEL), lambda i: (i, 0))],
        out_specs=pl.BlockSpec(memory_space=pl.ANY),
        out_shape=jax.ShapeDtypeStruct((n_alloc, D_MODEL), F32),
        scratch_shapes=[pltpu.SemaphoreType.DMA(())],
        compiler_params=_cparams(("arbitrary",)),
        name="dispatch",
    )(pos3, hn)


def _expert_kernel(ue_ref, ur_ref, un_ref, xs_ref, wg_ref, wu_ref, wd_ref, bg_ref, bu_ref, bd_ref,
                   ys_ref, xb_ref, h_ref, stage_ref, ybuf_ref, pend_ref, xsem, ysem):
    u = pl.program_id(0)
    s = pl.program_id(1)
    nu = pl.num_programs(0)
    nblk = un_ref[u]
    row0 = ur_ref[u]
    slot = u % 2

    def blk_rows(b):
        return pl.ds(pl.multiple_of(b * ROW_BLK, ROW_BLK), ROW_BLK)

    def for_row_blocks(fn):
        def pair(i, carry):
            fn(pl.ds(pl.multiple_of(i * (2 * ROW_BLK), 2 * ROW_BLK), 2 * ROW_BLK))
            return carry

        lax.fori_loop(0, lax.shift_right_logical(nblk, 1), pair, 0)

        @pl.when((nblk & 1) == 1)
        def _():
            fn(blk_rows(nblk - 1))

    def x_copy(unit, b):
        rows = pl.ds(pl.multiple_of(ur_ref[unit] + b * ROW_BLK, ROW_BLK), ROW_BLK)
        return pltpu.make_async_copy(xs_ref.at[rows, :], stage_ref, xsem)

    def y_copy(b, n, ys):
        rows = pl.ds(pl.multiple_of(row0 + b * ROW_BLK, ROW_BLK), ROW_BLK)
        cols = pl.ds(pl.multiple_of(n * FF_TILE, FF_TILE), FF_TILE)
        return pltpu.make_async_copy(ybuf_ref.at[ys, blk_rows(b), :], ys_ref.at[rows, cols], ysem)

    def drain_stores():
        def wait_one(i, carry):
            y_copy(0, 0, 0).wait()
            return carry

        lax.fori_loop(0, pend_ref[0], wait_one, 0)
        pend_ref[0] = 0

    @pl.when((u == 0) & (s == 0))
    def _():
        pend_ref[0] = 0

        def load(b, carry):
            cp = x_copy(0, b)
            cp.start()
            cp.wait()
            xb_ref[0, blk_rows(b), :] = stage_ref[...].astype(BF16)
            return carry

        lax.fori_loop(0, nblk, load, 0)

    nxt = jnp.minimum(u + 1, nu - 1)
    prefetch = (u + 1 < nu) & (s < un_ref[nxt])

    @pl.when(prefetch)
    def _():
        x_copy(nxt, s).start()

    @pl.when(s < NJ)
    def _():
        wg = wg_ref[...].astype(BF16)
        wu = wu_ref[...].astype(BF16)
        bg = bg_ref[...]
        bu = bu_ref[...]

        def up_rows(rows):
            xb = xb_ref[slot, rows, :]
            g = jnp.minimum(jnp.dot(xb, wg, preferred_element_type=F32) + bg, SWIGLU_LIMIT)
            up = jnp.clip(jnp.dot(xb, wu, preferred_element_type=F32) + bu,
                          -SWIGLU_LIMIT, SWIGLU_LIMIT)
            h_ref[s, rows, :] = ((up + 1.0) * (g * jax.nn.sigmoid(SWIGLU_ALPHA * g))).astype(BF16)

        for_row_blocks(up_rows)

    @pl.when(s >= NJ)
    def _():
        n = s - NJ
        ys = n % 2
        wd = wd_ref[...].astype(BF16)
        bd = bd_ref[...]

        def down_rows(rows):
            hb = jnp.concatenate([h_ref[t, rows, :] for t in range(NJ)], axis=1)
            ybuf_ref[ys, rows, :] = jnp.dot(hb, wd, preferred_element_type=F32) + bd

        for_row_blocks(down_rows)
        drain_stores()

        def issue(b, carry):
            y_copy(b, n, ys).start()
            return carry

        lax.fori_loop(0, nblk, issue, 0)
        pend_ref[0] = nblk

    @pl.when(prefetch)
    def _():
        x_copy(nxt, s).wait()
        xb_ref[1 - slot, blk_rows(s), :] = stage_ref[...].astype(BF16)

    @pl.when((u == nu - 1) & (s == 2 * NJ - 1))
    def _():
        drain_stores()


def _experts(xs, unit_e, unit_row0, unit_nblk, w_gate, w_up, w_down, b_gate, b_up, b_down):
    n_alloc = xs.shape[0]
    n_units = unit_e.shape[0]
    last = NJ - 1

    def up_tile(u, s, ue, ur, un):
        return (ue[u], 0, jnp.where(un[u] > 0, jnp.minimum(s, last), last))

    def down_tile(u, s, ue, ur, un):
        return (ue[u], 0, jnp.where(un[u] > 0, jnp.maximum(s - NJ, 0), last))

    grid_spec = pltpu.PrefetchScalarGridSpec(
        num_scalar_prefetch=3,
        grid=(n_units, 2 * NJ),
        in_specs=[
            pl.BlockSpec(memory_space=pl.ANY),
            pl.BlockSpec((None, D_MODEL, FF_TILE), up_tile),
            pl.BlockSpec((None, D_MODEL, FF_TILE), up_tile),
            pl.BlockSpec((None, D_FF, FF_TILE), down_tile),
            pl.BlockSpec((None, 1, FF_TILE), up_tile),
            pl.BlockSpec((None, 1, FF_TILE), up_tile),
            pl.BlockSpec((None, 1, FF_TILE), down_tile),
        ],
        out_specs=pl.BlockSpec(memory_space=pl.ANY),
        scratch_shapes=[pltpu.VMEM((2, SEG_ROWS, D_MODEL), BF16),
                        pltpu.VMEM((NJ, SEG_ROWS, FF_TILE), BF16),
                        pltpu.VMEM((ROW_BLK, D_MODEL), F32),
                        pltpu.VMEM((2, SEG_ROWS, FF_TILE), F32),
                        pltpu.SMEM((1,), I32),
                        pltpu.SemaphoreType.DMA(()),
                        pltpu.SemaphoreType.DMA(())],
    )
    return pl.pallas_call(
        _expert_kernel,
        grid_spec=grid_spec,
        out_shape=jax.ShapeDtypeStruct((n_alloc, D_MODEL), F32),
        compiler_params=_cparams(("arbitrary", "arbitrary")),
        name="experts",
    )(unit_e, unit_row0, unit_nblk, xs, w_gate, w_up, w_down,
      b_gate.reshape(N_EXPERTS, 1, D_FF), b_up.reshape(N_EXPERTS, 1, D_FF),
      b_down.reshape(N_EXPERTS, 1, D_MODEL))


COMB_ROWS = 128


def _combine_kernel(pos_ref, nxt_ref, h1_ref, wt_ref, ys_ref, o_ref, buf_ref, sems):
    i = pl.program_id(0)
    n = pl.num_programs(0)
    slot = i % 2

    def row_copy(p_ref, s, t, k):
        src = p_ref[0, 0, t * TOP_K + k]
        return pltpu.make_async_copy(ys_ref.at[pl.ds(src, 1), :],
                                     buf_ref.at[s, k, pl.ds(t, 1), :], sems.at[s])

    def issue_all(p_ref, s):
        def issue(j, carry):
            for tt in range(DMA_UNROLL):
                for k in range(TOP_K):
                    row_copy(p_ref, s, j * DMA_UNROLL + tt, k).start(priority=k % 2)
            return carry

        lax.fori_loop(0, COMB_ROWS // DMA_UNROLL, issue, 0)

    @pl.when(i == 0)
    def _():
        issue_all(pos_ref, 0)

    @pl.when(i + 1 < n)
    def _():
        issue_all(nxt_ref, 1 - slot)

    def drain(j, carry):
        for tt in range(DMA_UNROLL):
            for k in range(TOP_K):
                row_copy(pos_ref, slot, j * DMA_UNROLL + tt, k).wait()
        return carry

    lax.fori_loop(0, COMB_ROWS // DMA_UNROLL, drain, 0)

    wt = wt_ref[...]
    acc = h1_ref[...]
    for k in range(TOP_K):
        acc = acc + wt[:, k:k + 1] * buf_ref[slot, k]
    o_ref[...] = acc


def _combine(h1, wt, ys, pos):
    T = h1.shape[0]
    R = COMB_ROWS
    nsteps = T // R
    pos3 = pos.reshape(nsteps, 1, R * TOP_K)
    return pl.pallas_call(
        _combine_kernel,
        grid=(nsteps,),
        in_specs=[pl.BlockSpec((1, 1, R * TOP_K), lambda i: (i, 0, 0), memory_space=pltpu.SMEM),
                  pl.BlockSpec((1, 1, R * TOP_K), lambda i: (jnp.minimum(i + 1, nsteps - 1), 0, 0),
                               memory_space=pltpu.SMEM),
                  pl.BlockSpec((R, D_MODEL), lambda i: (i, 0)),
                  pl.BlockSpec((R, ROUTE_LANES), lambda i: (i, 0)),
                  pl.BlockSpec(memory_space=pl.ANY)],
        out_specs=pl.BlockSpec((R, D_MODEL), lambda i: (i, 0)),
        out_shape=jax.ShapeDtypeStruct((T, D_MODEL), F32),
        scratch_shapes=[pltpu.VMEM((2, TOP_K, R, D_MODEL), F32),
                        pltpu.SemaphoreType.DMA((2,))],
        compiler_params=_cparams(("arbitrary",)),
        name="combine",
    )(pos3, pos3, h1, wt, ys)


def _max_units(T):
    max_blocks = (T * TOP_K) // ROW_BLK + N_EXPERTS
    return max_blocks // SEG_BLKS + N_EXPERTS


def _routing_tables(counts, idx, rank, T):
    counts = counts.astype(I32)
    nblk = (counts + ROW_BLK - 1) // ROW_BLK
    pstart = (jnp.cumsum(nblk) - nblk) * ROW_BLK
    pos = jnp.take(pstart, idx, axis=0) + rank
    units = (nblk + SEG_BLKS - 1) // SEG_BLKS
    uend = jnp.cumsum(units)
    ustart = uend - units
    n_units = _max_units(T)
    u = jnp.arange(n_units, dtype=I32)
    total = uend[-1]
    ue = jnp.minimum(jnp.searchsorted(uend, u, side="right").astype(I32), N_EXPERTS - 1)
    local = u - jnp.take(ustart, ue)
    un = jnp.clip(jnp.take(nblk, ue) - local * SEG_BLKS, 0, SEG_BLKS)
    ur = jnp.take(pstart, ue) + local * SEG_ROWS
    live = u < total
    last_e = jnp.take(ue, jnp.maximum(total - 1, 0))
    ue = jnp.where(live, ue, last_e)
    un = jnp.where(live, un, 0)
    ur = jnp.where(live, ur, 0)
    return pos.astype(I32), ue, ur.astype(I32), un.astype(I32)


def kernel(x, norm1_g, w_in, w_gla_gate_up, b_gla_gate, gla_norm_g, q_norm_g, k_norm_g, rel_bias,
           b_branch_gate, w_branch_gla, w_branch_att, w_out, norm2_g, w_router, b_router,
           w_gate, b_gate, w_up, b_up, w_down, b_down):
    B, S, D = x.shape
    T = B * S
    h = x.reshape(T, D)
    for l in range(w_in.shape[0]):
        wl = w_in[l]
        xn, glr = _norm1(h, norm1_g[l].reshape(1, D), wl[:, COL_LR:COL_B].astype(BF16))
        proj_a = _proj(xn, w_in, l, COLS_A, "inproj_gla")
        proj_b = _proj(xn, wl[None, :, COL_B:].astype(BF16), 0, COLS_B, "inproj_att")
        y_gla = _gla(proj_a, glr, w_gla_gate_up[l], b_gla_gate[l].reshape(1, GLA_QK),
                     gla_norm_g[l].reshape(1, GLA_DV))
        y_att = _attention(proj_b, q_norm_g[l].reshape(1, ATT_HD), k_norm_g[l].reshape(1, ATT_HD),
                           _att_bias_diagonals(rel_bias[l]))
        merged = _merge(y_gla, y_att, w_branch_gla[l].astype(BF16), w_branch_att[l].astype(BF16),
                        proj_b, b_branch_gate[l])
        h1, hn, idx, wt, rank, counts = _route(
            merged, h, w_out[l].astype(BF16), norm2_g[l].reshape(1, D), w_router[l],
            b_router[l].reshape(1, N_EXPERTS))
        pos, ue, ur, un = _routing_tables(counts[0], idx[:, :TOP_K], rank[:, :TOP_K], T)
        n_alloc = ((T * TOP_K) // ROW_BLK + N_EXPERTS) * ROW_BLK
        xs = _dispatch(hn, pos.reshape(-1), n_alloc)
        ys = _experts(xs, ue, ur, un, w_gate[l], w_up[l], w_down[l], b_gate[l], b_up[l], b_down[l])
        h = _combine(h1, wt, ys, pos.reshape(-1))
    return h.reshape(B, S, D)
```

```python
import functools

import jax
import jax.numpy as jnp
import numpy as np
from jax import lax
from jax.experimental import pallas as pl
from jax.experimental.pallas import tpu as pltpu

F32 = jnp.float32
BF16 = jnp.bfloat16
I32 = jnp.int32

D_MODEL = 2048
CHUNK = 64
RMS_EPS = 1e-6
GLA_HEADS = 4
GLA_DK = 256
GLA_DV = 512
GLA_RANK = 16
GLA_TAU = 16.0
GLA_QK = GLA_HEADS * GLA_DK
GLA_V = GLA_HEADS * GLA_DV
ATT_HEADS = 8
ATT_HD = 128
ATT_W = ATT_HEADS * ATT_HD
ATT_LEFT = 8
REL_CLIP = 128
N_EXPERTS = 32
TOP_K = 4
D_FF = 2048
SWIGLU_LIMIT = 7.0
SWIGLU_ALPHA = 1.702

COLS_A = 2 * GLA_QK + 2 * GLA_V
COL_LR = COLS_A
COL_B = COLS_A + GLA_RANK
COLS_B = 3 * ATT_W + 2 * D_MODEL

VMEM_LIMIT = 56 * 1024 * 1024
NEG_BIG = -1e30

ROW_BLK = 256
SEG_BLKS = 8
SEG_ROWS = ROW_BLK * SEG_BLKS
FF_TILE = 256
NJ = D_FF // FF_TILE
assert D_MODEL // FF_TILE == NJ and SEG_BLKS <= NJ


def _cparams(sem):
    return pltpu.CompilerParams(dimension_semantics=sem, vmem_limit_bytes=VMEM_LIMIT)


def _norm1_kernel(x_ref, g_ref, wlr_ref, xn_ref, glr_ref):
    x = x_ref[...]
    ms = jnp.mean(x * x, axis=-1, keepdims=True)
    y = (x * lax.rsqrt(ms + RMS_EPS) * g_ref[...]).astype(BF16)
    xn_ref[...] = y
    glr_ref[...] = lax.dot_general(y, wlr_ref[...], (((1,), (1,)), ((), ())),
                                   preferred_element_type=F32)


def _norm1(x2, g, wlr):
    T = x2.shape[0]
    tm = 512
    return pl.pallas_call(
        _norm1_kernel,
        grid=(T // tm,),
        in_specs=[pl.BlockSpec((tm, D_MODEL), lambda i: (i, 0)),
                  pl.BlockSpec((1, D_MODEL), lambda i: (0, 0)),
                  pl.BlockSpec((GLA_RANK, D_MODEL), lambda i: (0, 0))],
        out_specs=[pl.BlockSpec((tm, D_MODEL), lambda i: (i, 0)),
                   pl.BlockSpec((tm, GLA_RANK), lambda i: (i, 0))],
        out_shape=[jax.ShapeDtypeStruct((T, D_MODEL), BF16),
                   jax.ShapeDtypeStruct((T, GLA_RANK), F32)],
        compiler_params=_cparams(("parallel",)),
        name="norm1",
    )(x2, g, wlr)


def _proj_kernel(x_ref, w_ref, o_ref, wb_ref):
    @pl.when(pl.program_id(1) == 0)
    def _():
        wb_ref[...] = w_ref[...].astype(BF16)

    o_ref[...] = lax.dot_general(x_ref[...], wb_ref[...], (((1,), (1,)), ((), ())),
                                 preferred_element_type=F32).astype(o_ref.dtype)


def _proj(xn, wt, col0, n_cols, name):
    T, K = xn.shape
    tm, tn = 1024, 1024
    assert col0 % 8 == 0
    return pl.pallas_call(
        _proj_kernel,
        grid=(n_cols // tn, T // tm),
        in_specs=[pl.BlockSpec((tm, K), lambda j, i: (i, 0)),
                  pl.BlockSpec((pl.Element(tn), pl.Element(K)),
                               lambda j, i: (pl.multiple_of(col0 + j * tn, 8), 0))],
        out_specs=pl.BlockSpec((tm, tn), lambda j, i: (i, j)),
        out_shape=jax.ShapeDtypeStruct((T, n_cols), BF16),
        scratch_shapes=[pltpu.VMEM((tn, K), BF16)],
        compiler_params=_cparams(("arbitrary", "arbitrary")),
        name=name,
    )(xn, wt)


GLA_ROWS = 256


def _gla_kernel(q_ref, k_ref, v_ref, r_ref, glr_ref, wup_ref, bg_ref, ng_ref, o_ref, st_ref):
    @pl.when(pl.program_id(0) == 0)
    def _():
        st_ref[...] = jnp.zeros_like(st_ref)

    z = jnp.dot(glr_ref[...], wup_ref[...], preferred_element_type=F32,
                precision=lax.Precision.HIGHEST) + bg_ref[...]
    log_a = (jnp.minimum(z, 0.0) - jnp.log(1.0 + jnp.exp(-jnp.abs(z)))) * (1.0 / GLA_TAU)

    ri = lax.broadcasted_iota(I32, (CHUNK, CHUNK), 0)
    ci = lax.broadcasted_iota(I32, (CHUNK, CHUNK), 1)
    causal = ci <= ri
    tril = jnp.where(causal, 1.0, 0.0).astype(BF16)
    scale = GLA_DK ** -0.5
    ng = ng_ref[...]

    for c in range(GLA_ROWS // CHUNK):
        rows = slice(c * CHUNK, (c + 1) * CHUNK)
        la = log_a[rows]
        la_hi = la.astype(BF16)
        la_lo = (la - la_hi.astype(F32)).astype(BF16)
        bcum_all = (jnp.dot(tril, la_hi, preferred_element_type=F32)
                    + jnp.dot(tril, la_lo, preferred_element_type=F32))
        for h in range(GLA_HEADS):
            kcols = slice(h * GLA_DK, (h + 1) * GLA_DK)
            vcols = slice(h * GLA_DV, (h + 1) * GLA_DV)
            bcum = bcum_all[:, kcols]
            blast = bcum[CHUNK - 1:CHUNK, :]
            qc = q_ref[rows, kcols].astype(F32) * scale
            kc = k_ref[rows, kcols].astype(F32)
            q_dec = (qc * jnp.exp(bcum)).astype(BF16)
            k_inv = (kc * jnp.exp(-bcum)).astype(BF16)
            k_end = (kc * jnp.exp(blast - bcum)).astype(BF16)
            vv = v_ref[rows, vcols]
            a = lax.dot_general(q_dec, k_inv, (((1,), (1,)), ((), ())), preferred_element_type=F32)
            a = jnp.where(causal, a, 0.0).astype(BF16)
            st = st_ref[h]
            o = jnp.dot(a, vv, preferred_element_type=F32)
            o = o + lax.dot_general(q_dec, st.astype(BF16), (((1,), (1,)), ((), ())),
                                    preferred_element_type=F32)
            st_ref[h] = st * jnp.exp(blast) + lax.dot_general(
                vv, k_end, (((0,), (0,)), ((), ())), preferred_element_type=F32)
            ms = jnp.mean(o * o, axis=-1, keepdims=True)
            y = o * lax.rsqrt(ms + RMS_EPS) * ng
            rr = r_ref[rows, vcols].astype(F32)
            o_ref[rows, vcols] = (y * (rr * jax.nn.sigmoid(rr))).astype(o_ref.dtype)


def _gla(proj_a, glr, wup, bg, ng):
    T = proj_a.shape[0]
    R = GLA_ROWS
    return pl.pallas_call(
        _gla_kernel,
        grid=(T // R,),
        in_specs=[pl.BlockSpec((R, GLA_QK), lambda n: (n, 0)),
                  pl.BlockSpec((R, GLA_QK), lambda n: (n, 1)),
                  pl.BlockSpec((R, GLA_V), lambda n: (n, 1)),
                  pl.BlockSpec((R, GLA_V), lambda n: (n, 2)),
                  pl.BlockSpec((R, GLA_RANK), lambda n: (n, 0)),
                  pl.BlockSpec((GLA_RANK, GLA_QK), lambda n: (0, 0)),
                  pl.BlockSpec((1, GLA_QK), lambda n: (0, 0)),
                  pl.BlockSpec((1, GLA_DV), lambda n: (0, 0))],
        out_specs=pl.BlockSpec((R, GLA_V), lambda n: (n, 0)),
        out_shape=jax.ShapeDtypeStruct((T, GLA_V), BF16),
        scratch_shapes=[pltpu.VMEM((GLA_HEADS, GLA_DV, GLA_DK), F32)],
        compiler_params=_cparams(("arbitrary",)),
        name="gla",
    )(proj_a, proj_a, proj_a, proj_a, glr, wup, bg, ng)


ATT_QROWS = 256
ATT_PAD = ATT_LEFT * CHUNK
ATT_BAND = ATT_PAD + ATT_QROWS


ATT_EXT = 1024


def _att_bias_diagonals(rel_bias):
    m = np.arange(ATT_EXT)
    m = np.where(m < ATT_BAND, m, m - ATT_EXT)
    rel = np.clip(ATT_PAD - m, -REL_CLIP, REL_CLIP) + REL_CLIP
    return rel_bias.astype(F32)[:, rel].reshape(ATT_HEADS, 1, ATT_EXT)


def _att_kernel(q_ref, k_ref, v_ref, qg_ref, kg_ref, ext_ref, o_ref, kn_ref, vp_ref, tab_ref):
    qb = pl.program_id(1)

    @pl.when(qb == 0)
    def _():
        ext = jnp.broadcast_to(ext_ref[...], (ATT_QROWS, ATT_EXT))
        tab = pltpu.roll(ext, 0, 1, stride=1, stride_axis=0)[:, :ATT_BAND]
        qc = lax.broadcasted_iota(I32, (ATT_QROWS, ATT_BAND), 0) // CHUNK
        kc = lax.broadcasted_iota(I32, (ATT_QROWS, ATT_BAND), 1) // CHUNK
        tab_ref[...] = jnp.where((kc >= qc) & (kc <= qc + ATT_LEFT), tab, NEG_BIG)
        kk = k_ref[...].astype(F32)
        ms = jnp.mean(kk * kk, axis=-1, keepdims=True)
        kn = kk * lax.rsqrt(ms + RMS_EPS) * kg_ref[...]
        kn_ref[0:ATT_PAD, :] = jnp.zeros((ATT_PAD, ATT_HD), BF16)
        kn_ref[ATT_PAD:, :] = kn.astype(BF16)
        vp_ref[0:ATT_PAD, :] = jnp.zeros((ATT_PAD, ATT_HD), BF16)
        vp_ref[ATT_PAD:, :] = v_ref[...]

    q = q_ref[...].astype(F32)
    ms = jnp.mean(q * q, axis=-1, keepdims=True)
    qn = (q * lax.rsqrt(ms + RMS_EPS) * qg_ref[...] * (ATT_HD ** -0.5)).astype(BF16)
    start = pl.multiple_of(qb * ATT_QROWS, ATT_QROWS)
    kb = kn_ref[pl.ds(start, ATT_BAND), :]
    vb = vp_ref[pl.ds(start, ATT_BAND), :]
    s = lax.dot_general(qn, kb, (((1,), (1,)), ((), ())), preferred_element_type=F32)
    s = s + tab_ref[...]
    kpos = start - ATT_PAD + lax.broadcasted_iota(I32, (1, ATT_BAND), 1)
    s = jnp.where(kpos >= 0, s, NEG_BIG)
    m = jnp.max(s, axis=-1, keepdims=True)
    p = jnp.exp(s - m)
    l = jnp.sum(p, axis=-1, keepdims=True)
    o = jnp.dot(p.astype(BF16), vb, preferred_element_type=F32)
    o_ref[...] = (o / l).astype(o_ref.dtype)


def _attention(proj_b, qg, kg, ext):
    T = proj_b.shape[0]
    R = ATT_QROWS
    return pl.pallas_call(
        _att_kernel,
        grid=(ATT_HEADS, T // R),
        in_specs=[pl.BlockSpec((R, ATT_HD), lambda h, n: (n, h)),
                  pl.BlockSpec((T, ATT_HD), lambda h, n: (0, ATT_HEADS + h)),
                  pl.BlockSpec((T, ATT_HD), lambda h, n: (0, 2 * ATT_HEADS + h)),
                  pl.BlockSpec((1, ATT_HD), lambda h, n: (0, 0)),
                  pl.BlockSpec((1, ATT_HD), lambda h, n: (0, 0)),
                  pl.BlockSpec((None, 1, ATT_EXT), lambda h, n: (h, 0, 0))],
        out_specs=pl.BlockSpec((R, ATT_HD), lambda h, n: (n, h)),
        out_shape=jax.ShapeDtypeStruct((T, ATT_W), BF16),
        scratch_shapes=[pltpu.VMEM((T + ATT_PAD, ATT_HD), BF16),
                        pltpu.VMEM((T + ATT_PAD, ATT_HD), BF16),
                        pltpu.VMEM((R, ATT_BAND), F32)],
        compiler_params=_cparams(("arbitrary", "arbitrary")),
        name="chunk_attention",
    )(proj_b, proj_b, proj_b, qg, kg, ext)


def _merge_kernel(yg_ref, ya_ref, wg_ref, wa_ref, gg_ref, ga_ref, bb_ref, o_ref):
    pg = jnp.dot(yg_ref[...], wg_ref[...], preferred_element_type=F32)
    pa = jnp.dot(ya_ref[...], wa_ref[...], preferred_element_type=F32)
    bb = bb_ref[...]
    sg = jax.nn.sigmoid(gg_ref[...].astype(F32) + bb[0:1, :])
    sa = jax.nn.sigmoid(ga_ref[...].astype(F32) + bb[1:2, :])
    o_ref[...] = (sg * pg + sa * pa).astype(o_ref.dtype)


def _merge(y_gla, y_att, wg, wa, proj_b, bb):
    T = y_gla.shape[0]
    tm, tn = 1024, 1024
    gate0 = 3 * ATT_W // tn
    return pl.pallas_call(
        _merge_kernel,
        grid=(D_MODEL // tn, T // tm),
        in_specs=[pl.BlockSpec((tm, GLA_V), lambda j, i: (i, 0)),
                  pl.BlockSpec((tm, ATT_W), lambda j, i: (i, 0)),
                  pl.BlockSpec((GLA_V, tn), lambda j, i: (0, j)),
                  pl.BlockSpec((ATT_W, tn), lambda j, i: (0, j)),
                  pl.BlockSpec((tm, tn), lambda j, i: (i, gate0 + j)),
                  pl.BlockSpec((tm, tn), lambda j, i: (i, gate0 + D_MODEL // tn + j)),
                  pl.BlockSpec((2, tn), lambda j, i: (0, j))],
        out_specs=pl.BlockSpec((tm, tn), lambda j, i: (i, j)),
        out_shape=jax.ShapeDtypeStruct((T, D_MODEL), BF16),
        compiler_params=_cparams(("arbitrary", "arbitrary")),
        name="merge",
    )(y_gla, y_att, wg, wa, proj_b, proj_b, bb)


ROUTE_ROWS = 256
ROUTE_LANES = 128


def _route_kernel(m_ref, x_ref, wo_ref, g2_ref, wr_ref, br_ref,
                  h1_ref, hn_ref, idx_ref, wt_ref, rank_ref, cnt_ref, run_ref):
    @pl.when(pl.program_id(0) == 0)
    def _():
        run_ref[...] = jnp.zeros_like(run_ref)

    h1 = x_ref[...] + jnp.dot(m_ref[...], wo_ref[...], preferred_element_type=F32)
    h1_ref[...] = h1
    ms = jnp.mean(h1 * h1, axis=-1, keepdims=True)
    hn = h1 * lax.rsqrt(ms + RMS_EPS) * g2_ref[...]
    hn_ref[...] = hn
    wr = wr_ref[...]
    hn_hi = hn.astype(BF16)
    hn_lo = (hn - hn_hi.astype(F32)).astype(BF16)
    wr_hi = wr.astype(BF16)
    wr_lo = (wr - wr_hi.astype(F32)).astype(BF16)
    logits = (jnp.dot(hn_hi, wr_hi, preferred_element_type=F32)
              + jnp.dot(hn_hi, wr_lo, preferred_element_type=F32)
              + jnp.dot(hn_lo, wr_hi, preferred_element_type=F32)) + br_ref[...]

    R = ROUTE_ROWS
    lanes = lax.broadcasted_iota(I32, (R, N_EXPERTS), 1)
    work = logits
    vals, sels, idxs = [], [], []
    for _ in range(TOP_K):
        m = jnp.max(work, axis=-1, keepdims=True)
        idx = jnp.min(jnp.where(work == m, lanes, N_EXPERTS), axis=-1, keepdims=True)
        sel = lanes == idx
        vals.append(m)
        idxs.append(idx)
        sels.append(sel)
        work = jnp.where(sel, -jnp.inf, work)
    es = [jnp.exp(v - vals[0]) for v in vals]
    denom = es[0] + es[1] + es[2] + es[3]

    hot = jnp.zeros((R, N_EXPERTS), F32)
    for sel in sels:
        hot = jnp.where(sel, 1.0, hot)
    ri = lax.broadcasted_iota(I32, (R, R), 0)
    ci = lax.broadcasted_iota(I32, (R, R), 1)
    strict = jnp.where(ci < ri, 1.0, 0.0).astype(BF16)
    before = jnp.dot(strict, hot.astype(BF16), preferred_element_type=F32) + run_ref[...]
    run_new = run_ref[...] + jnp.sum(hot, axis=0, keepdims=True)
    run_ref[...] = run_new
    cnt_ref[...] = run_new

    ol = lax.broadcasted_iota(I32, (R, ROUTE_LANES), 1)
    idx_o = jnp.zeros((R, ROUTE_LANES), I32)
    wt_o = jnp.zeros((R, ROUTE_LANES), F32)
    rank_o = jnp.zeros((R, ROUTE_LANES), I32)
    for k in range(TOP_K):
        rk = jnp.sum(jnp.where(sels[k], before, 0.0), axis=-1, keepdims=True).astype(I32)
        idx_o = jnp.where(ol == k, idxs[k], idx_o)
        wt_o = jnp.where(ol == k, es[k] / denom, wt_o)
        rank_o = jnp.where(ol == k, rk, rank_o)
    idx_ref[...] = idx_o
    wt_ref[...] = wt_o
    rank_ref[...] = rank_o


def _route(merged, x2, wo, g2, wr, br):
    T = x2.shape[0]
    R = ROUTE_ROWS
    row = lambda i: (i, 0)
    fixed = lambda i: (0, 0)
    return pl.pallas_call(
        _route_kernel,
        grid=(T // R,),
        in_specs=[pl.BlockSpec((R, D_MODEL), row),
                  pl.BlockSpec((R, D_MODEL), row),
                  pl.BlockSpec((D_MODEL, D_MODEL), fixed),
                  pl.BlockSpec((1, D_MODEL), fixed),
                  pl.BlockSpec((D_MODEL, N_EXPERTS), fixed),
                  pl.BlockSpec((1, N_EXPERTS), fixed)],
        out_specs=[pl.BlockSpec((R, D_MODEL), row),
                   pl.BlockSpec((R, D_MODEL), row),
                   pl.BlockSpec((R, ROUTE_LANES), row),
                   pl.BlockSpec((R, ROUTE_LANES), row),
                   pl.BlockSpec((R, ROUTE_LANES), row),
                   pl.BlockSpec((1, N_EXPERTS), fixed)],
        out_shape=[jax.ShapeDtypeStruct((T, D_MODEL), F32),
                   jax.ShapeDtypeStruct((T, D_MODEL), F32),
                   jax.ShapeDtypeStruct((T, ROUTE_LANES), I32),
                   jax.ShapeDtypeStruct((T, ROUTE_LANES), F32),
                   jax.ShapeDtypeStruct((T, ROUTE_LANES), I32),
                   jax.ShapeDtypeStruct((1, N_EXPERTS), F32)],
        scratch_shapes=[pltpu.VMEM((1, N_EXPERTS), F32)],
        compiler_params=_cparams(("arbitrary",)),
        name="outproj_route",
    )(merged, x2, wo, g2, wr, br)


DISP_ROWS = 512
DMA_UNROLL = 8


def _dispatch_kernel(pos_ref, hn_ref, xs_ref, sem):
    def row_copy(t, k):
        dst = pos_ref[0, 0, t * TOP_K + k]
        return pltpu.make_async_copy(hn_ref.at[pl.ds(t, 1), :], xs_ref.at[pl.ds(dst, 1), :], sem)

    def issue(i, carry):
        for tt in range(DMA_UNROLL):
            for k in range(TOP_K):
                row_copy(i * DMA_UNROLL + tt, k).start(priority=k % 2)
        return carry

    lax.fori_loop(0, DISP_ROWS // DMA_UNROLL, issue, 0)

    def drain(i, carry):
        for tt in range(DMA_UNROLL):
            for k in range(TOP_K):
                row_copy(i * DMA_UNROLL + tt, k).wait()
        return carry

    lax.fori_loop(0, DISP_ROWS // DMA_UNROLL, drain, 0)


def _dispatch(hn, pos, n_alloc):
    T = hn.shape[0]
    R = DISP_ROWS
    pos3 = pos.reshape(T // R, 1, R * TOP_K)
    return pl.pallas_call(
        _dispatch_kernel,
        grid=(T // R,),
        in_specs=[pl.BlockSpec((1, 1, R * TOP_K), lambda i: (i, 0, 0), memory_space=pltpu.SMEM),
                  pl.BlockSpec((R, D_MODEL), lambda i: (i, 0))],
        out_specs=pl.BlockSpec(memory_space=pl.ANY),
        out_shape=jax.ShapeDtypeStruct((n_alloc, D_MODEL), F32),
        scratch_shapes=[pltpu.SemaphoreType.DMA(())],
        compiler_params=_cparams(("arbitrary",)),
        name="dispatch",
    )(pos3, hn)


def _expert_kernel(ue_ref, ur_ref, un_ref, xs_ref, wg_ref, wu_ref, wd_ref, bg_ref, bu_ref, bd_ref,
                   ys_ref, xb_ref, h_ref, stage_ref, ybuf_ref, pend_ref, xsem, ysem):
    u = pl.program_id(0)
    s = pl.program_id(1)
    nu = pl.num_programs(0)
    nblk = un_ref[u]
    row0 = ur_ref[u]
    slot = u % 2

    def blk_rows(b):
        return pl.ds(pl.multiple_of(b * ROW_BLK, ROW_BLK), ROW_BLK)

    def for_row_blocks(fn):
        def pair(i, carry):
            fn(pl.ds(pl.multiple_of(i * (2 * ROW_BLK), 2 * ROW_BLK), 2 * ROW_BLK))
            return carry

        lax.fori_loop(0, lax.shift_right_logical(nblk, 1), pair, 0)

        @pl.when((nblk & 1) == 1)
        def _():
            fn(blk_rows(nblk - 1))

    def x_copy(unit, b):
        rows = pl.ds(pl.multiple_of(ur_ref[unit] + b * ROW_BLK, ROW_BLK), ROW_BLK)
        return pltpu.make_async_copy(xs_ref.at[rows, :], stage_ref, xsem)

    def y_copy(b, n, ys):
        rows = pl.ds(pl.multiple_of(row0 + b * ROW_BLK, ROW_BLK), ROW_BLK)
        cols = pl.ds(pl.multiple_of(n * FF_TILE, FF_TILE), FF_TILE)
        return pltpu.make_async_copy(ybuf_ref.at[ys, blk_rows(b), :], ys_ref.at[rows, cols], ysem)

    def drain_stores():
        def wait_one(i, carry):
            y_copy(0, 0, 0).wait()
            return carry

        lax.fori_loop(0, pend_ref[0], wait_one, 0)
        pend_ref[0] = 0

    @pl.when((u == 0) & (s == 0))
    def _():
        pend_ref[0] = 0

        def load(b, carry):
            cp = x_copy(0, b)
            cp.start()
            cp.wait()
            xb_ref[0, blk_rows(b), :] = stage_ref[...].astype(BF16)
            return carry

        lax.fori_loop(0, nblk, load, 0)

    nxt = jnp.minimum(u + 1, nu - 1)
    prefetch = (u + 1 < nu) & (s < un_ref[nxt])

    @pl.when(prefetch)
    def _():
        x_copy(nxt, s).start()

    @pl.when(s < NJ)
    def _():
        wg = wg_ref[...].astype(BF16)
        wu = wu_ref[...].astype(BF16)
        bg = bg_ref[...]
        bu = bu_ref[...]

        def up_rows(rows):
            xb = xb_ref[slot, rows, :]
            g = jnp.minimum(jnp.dot(xb, wg, preferred_element_type=F32) + bg, SWIGLU_LIMIT)
            up = jnp.clip(jnp.dot(xb, wu, preferred_element_type=F32) + bu,
                          -SWIGLU_LIMIT, SWIGLU_LIMIT)
            h_ref[s, rows, :] = ((up + 1.0) * (g * jax.nn.sigmoid(SWIGLU_ALPHA * g))).astype(BF16)

        for_row_blocks(up_rows)

    @pl.when(s >= NJ)
    def _():
        n = s - NJ
        ys = n % 2
        wd = wd_ref[...].astype(BF16)
        bd = bd_ref[...]

        def down_rows(rows):
            hb = jnp.concatenate([h_ref[t, rows, :] for t in range(NJ)], axis=1)
            ybuf_ref[ys, rows, :] = jnp.dot(hb, wd, preferred_element_type=F32) + bd

        for_row_blocks(down_rows)
        drain_stores()

        def issue(b, carry):
            y_copy(b, n, ys).start()
            return carry

        lax.fori_loop(0, nblk, issue, 0)
        pend_ref[0] = nblk

    @pl.when(prefetch)
    def _():
        x_copy(nxt, s).wait()
        xb_ref[1 - slot, blk_rows(s), :] = stage_ref[...].astype(BF16)

    @pl.when((u == nu - 1) & (s == 2 * NJ - 1))
    def _():
        drain_stores()


def _experts(xs, unit_e, unit_row0, unit_nblk, w_gate, w_up, w_down, b_gate, b_up, b_down):
    n_alloc = xs.shape[0]
    n_units = unit_e.shape[0]
    last = NJ - 1

    def up_tile(u, s, ue, ur, un):
        return (ue[u], 0, jnp.where(un[u] > 0, jnp.minimum(s, last), last))

    def down_tile(u, s, ue, ur, un):
        return (ue[u], 0, jnp.where(un[u] > 0, jnp.maximum(s - NJ, 0), last))

    grid_spec = pltpu.PrefetchScalarGridSpec(
        num_scalar_prefetch=3,
        grid=(n_units, 2 * NJ),
        in_specs=[
            pl.BlockSpec(memory_space=pl.ANY),
            pl.BlockSpec((None, D_MODEL, FF_TILE), up_tile),
            pl.BlockSpec((None, D_MODEL, FF_TILE), up_tile),
            pl.BlockSpec((None, D_FF, FF_TILE), down_tile),
            pl.BlockSpec((None, 1, FF_TILE), up_tile),
            pl.BlockSpec((None, 1, FF_TILE), up_tile),
            pl.BlockSpec((None, 1, FF_TILE), down_tile),
        ],
        out_specs=pl.BlockSpec(memory_space=pl.ANY),
        scratch_shapes=[pltpu.VMEM((2, SEG_ROWS, D_MODEL), BF16),
                        pltpu.VMEM((NJ, SEG_ROWS, FF_TILE), BF16),
                        pltpu.VMEM((ROW_BLK, D_MODEL), F32),
                        pltpu.VMEM((2, SEG_ROWS, FF_TILE), F32),
                        pltpu.SMEM((1,), I32),
                        pltpu.SemaphoreType.DMA(()),
                        pltpu.SemaphoreType.DMA(())],
    )
    return pl.pallas_call(
        _expert_kernel,
        grid_spec=grid_spec,
        out_shape=jax.ShapeDtypeStruct((n_alloc, D_MODEL), F32),
        compiler_params=_cparams(("arbitrary", "arbitrary")),
        name="experts",
    )(unit_e, unit_row0, unit_nblk, xs, w_gate, w_up, w_down,
      b_gate.reshape(N_EXPERTS, 1, D_FF), b_up.reshape(N_EXPERTS, 1, D_FF),
      b_down.reshape(N_EXPERTS, 1, D_MODEL))


COMB_ROWS = 128


def _combine_kernel(pos_ref, nxt_ref, h1_ref, wt_ref, ys_ref, o_ref, buf_ref, sems):
    i = pl.program_id(0)
    n = pl.num_programs(0)
    slot = i % 2

    def row_copy(p_ref, s, t, k):
        src = p_ref[0, 0, t * TOP_K + k]
        return pltpu.make_async_copy(ys_ref.at[pl.ds(src, 1), :],
                                     buf_ref.at[s, k, pl.ds(t, 1), :], sems.at[s])

    def issue_all(p_ref, s):
        def issue(j, carry):
            for tt in range(DMA_UNROLL):
                for k in range(TOP_K):
                    row_copy(p_ref, s, j * DMA_UNROLL + tt, k).start(priority=k % 2)
            return carry

        lax.fori_loop(0, COMB_ROWS // DMA_UNROLL, issue, 0)

    @pl.when(i == 0)
    def _():
        issue_all(pos_ref, 0)

    @pl.when(i + 1 < n)
    def _():
        issue_all(nxt_ref, 1 - slot)

    def drain(j, carry):
        for tt in range(DMA_UNROLL):
            for k in range(TOP_K):
                row_copy(pos_ref, slot, j * DMA_UNROLL + tt, k).wait()
        return carry

    lax.fori_loop(0, COMB_ROWS // DMA_UNROLL, drain, 0)

    wt = wt_ref[...]
    acc = h1_ref[...]
    for k in range(TOP_K):
        acc = acc + wt[:, k:k + 1] * buf_ref[slot, k]
    o_ref[...] = acc


def _combine(h1, wt, ys, pos):
    T = h1.shape[0]
    R = COMB_ROWS
    nsteps = T // R
    pos3 = pos.reshape(nsteps, 1, R * TOP_K)
    return pl.pallas_call(
        _combine_kernel,
        grid=(nsteps,),
        in_specs=[pl.BlockSpec((1, 1, R * TOP_K), lambda i: (i, 0, 0), memory_space=pltpu.SMEM),
                  pl.BlockSpec((1, 1, R * TOP_K), lambda i: (jnp.minimum(i + 1, nsteps - 1), 0, 0),
                               memory_space=pltpu.SMEM),
                  pl.BlockSpec((R, D_MODEL), lambda i: (i, 0)),
                  pl.BlockSpec((R, ROUTE_LANES), lambda i: (i, 0)),
                  pl.BlockSpec(memory_space=pl.ANY)],
        out_specs=pl.BlockSpec((R, D_MODEL), lambda i: (i, 0)),
        out_shape=jax.ShapeDtypeStruct((T, D_MODEL), F32),
        scratch_shapes=[pltpu.VMEM((2, TOP_K, R, D_MODEL), F32),
                        pltpu.SemaphoreType.DMA((2,))],
        compiler_params=_cparams(("arbitrary",)),
        name="combine",
    )(pos3, pos3, h1, wt, ys)


def _max_units(T):
    max_blocks = (T * TOP_K) // ROW_BLK + N_EXPERTS
    return max_blocks // SEG_BLKS + N_EXPERTS


def _routing_tables(counts, idx, rank, T):
    counts = counts.astype(I32)
    nblk = (counts + ROW_BLK - 1) // ROW_BLK
    pstart = (jnp.cumsum(nblk) - nblk) * ROW_BLK
    hot = idx[..., None] == jnp.arange(N_EXPERTS, dtype=I32)
    pos = jnp.sum(jnp.where(hot, pstart, 0), axis=-1) + rank
    units = (nblk + SEG_BLKS - 1) // SEG_BLKS
    uend = jnp.cumsum(units)
    ustart = uend - units
    n_units = _max_units(T)
    u = jnp.arange(n_units, dtype=I32)
    total = uend[-1]
    ue = jnp.minimum(jnp.searchsorted(uend, u, side="right").astype(I32), N_EXPERTS - 1)
    local = u - jnp.take(ustart, ue)
    un = jnp.clip(jnp.take(nblk, ue) - local * SEG_BLKS, 0, SEG_BLKS)
    ur = jnp.take(pstart, ue) + local * SEG_ROWS
    live = u < total
    last_e = jnp.take(ue, jnp.maximum(total - 1, 0))
    ue = jnp.where(live, ue, last_e)
    un = jnp.where(live, un, 0)
    ur = jnp.where(live, ur, 0)
    return pos.astype(I32), ue, ur.astype(I32), un.astype(I32)


def kernel(x, norm1_g, w_in, w_gla_gate_up, b_gla_gate, gla_norm_g, q_norm_g, k_norm_g, rel_bias,
           b_branch_gate, w_branch_gla, w_branch_att, w_out, norm2_g, w_router, b_router,
           w_gate, b_gate, w_up, b_up, w_down, b_down):
    B, S, D = x.shape
    T = B * S
    h = x.reshape(T, D)
    wt_in = jnp.swapaxes(w_in, 1, 2)
    for l in range(w_in.shape[0]):
        xn, glr = _norm1(h, norm1_g[l].reshape(1, D), wt_in[l, COL_LR:COL_B, :].astype(BF16))
        proj_a = _proj(xn, wt_in[l], 0, COLS_A, "inproj_gla")
        proj_b = _proj(xn, wt_in[l], COL_B, COLS_B, "inproj_att")
        y_gla = _gla(proj_a, glr, w_gla_gate_up[l], b_gla_gate[l].reshape(1, GLA_QK),
                     gla_norm_g[l].reshape(1, GLA_DV))
        y_att = _attention(proj_b, q_norm_g[l].reshape(1, ATT_HD), k_norm_g[l].reshape(1, ATT_HD),
                           _att_bias_diagonals(rel_bias[l]))
        merged = _merge(y_gla, y_att, w_branch_gla[l].astype(BF16), w_branch_att[l].astype(BF16),
                        proj_b, b_branch_gate[l])
        h1, hn, idx, wt, rank, counts = _route(
            merged, h, w_out[l].astype(BF16), norm2_g[l].reshape(1, D), w_router[l],
            b_router[l].reshape(1, N_EXPERTS))
        pos, ue, ur, un = _routing_tables(counts[0], idx[:, :TOP_K], rank[:, :TOP_K], T)
        n_alloc = ((T * TOP_K) // ROW_BLK + N_EXPERTS) * ROW_BLK
        xs = _dispatch(hn, pos.reshape(-1), n_alloc)
        ys = _experts(xs, ue, ur, un, w_gate[l], w_up[l], w_down[l], b_gate[l], b_up[l], b_down[l])
        h = _combine(h1, wt, ys, pos.reshape(-1))
    return h.reshape(B, S, D)
```

```python
import functools

import jax
import jax.numpy as jnp
import numpy as np
from jax import lax
from jax.experimental import pallas as pl
from jax.experimental.pallas import tpu as pltpu

F32 = jnp.float32
BF16 = jnp.bfloat16
I32 = jnp.int32

D_MODEL = 2048
CHUNK = 64
RMS_EPS = 1e-6
GLA_HEADS = 4
GLA_DK = 256
GLA_DV = 512
GLA_RANK = 16
GLA_TAU = 16.0
GLA_QK = GLA_HEADS * GLA_DK
GLA_V = GLA_HEADS * GLA_DV
ATT_HEADS = 8
ATT_HD = 128
ATT_W = ATT_HEADS * ATT_HD
ATT_LEFT = 8
REL_CLIP = 128
N_EXPERTS = 32
TOP_K = 4
D_FF = 2048
SWIGLU_LIMIT = 7.0
SWIGLU_ALPHA = 1.702

COLS_A = 2 * GLA_QK + 2 * GLA_V
COL_LR = COLS_A
COL_B = COLS_A + GLA_RANK
COLS_B = 3 * ATT_W + 2 * D_MODEL

VMEM_LIMIT = 56 * 1024 * 1024
NEG_BIG = -1e30

ROW_BLK = 256
SEG_BLKS = 5
SEG_ROWS = ROW_BLK * SEG_BLKS
FF_TILE = 512
NJ = D_FF // FF_TILE
assert D_MODEL // FF_TILE == NJ and SEG_BLKS <= 2 * NJ


def _cparams(sem):
    return pltpu.CompilerParams(dimension_semantics=sem, vmem_limit_bytes=VMEM_LIMIT)


def _norm1_kernel(x_ref, g_ref, wlr_ref, xn_ref, glr_ref):
    x = x_ref[...]
    ms = jnp.mean(x * x, axis=-1, keepdims=True)
    y = (x * lax.rsqrt(ms + RMS_EPS) * g_ref[...]).astype(BF16)
    xn_ref[...] = y
    glr_ref[...] = lax.dot_general(y, wlr_ref[...], (((1,), (1,)), ((), ())),
                                   preferred_element_type=F32)


def _norm1(x2, g, wlr):
    T = x2.shape[0]
    tm = 512
    return pl.pallas_call(
        _norm1_kernel,
        grid=(T // tm,),
        in_specs=[pl.BlockSpec((tm, D_MODEL), lambda i: (i, 0)),
                  pl.BlockSpec((1, D_MODEL), lambda i: (0, 0)),
                  pl.BlockSpec((GLA_RANK, D_MODEL), lambda i: (0, 0))],
        out_specs=[pl.BlockSpec((tm, D_MODEL), lambda i: (i, 0)),
                   pl.BlockSpec((tm, GLA_RANK), lambda i: (i, 0))],
        out_shape=[jax.ShapeDtypeStruct((T, D_MODEL), BF16),
                   jax.ShapeDtypeStruct((T, GLA_RANK), F32)],
        compiler_params=_cparams(("parallel",)),
        name="norm1",
    )(x2, g, wlr)


def _proj_kernel(x_ref, w_ref, o_ref, wb_ref):
    @pl.when(pl.program_id(1) == 0)
    def _():
        wb_ref[...] = w_ref[...].astype(BF16)

    o_ref[...] = lax.dot_general(x_ref[...], wb_ref[...], (((1,), (1,)), ((), ())),
                                 preferred_element_type=F32).astype(o_ref.dtype)


def _proj(xn, wt, col0, n_cols, name):
    T, K = xn.shape
    tm, tn = 1024, 1024
    assert col0 % 8 == 0
    return pl.pallas_call(
        _proj_kernel,
        grid=(n_cols // tn, T // tm),
        in_specs=[pl.BlockSpec((tm, K), lambda j, i: (i, 0)),
                  pl.BlockSpec((pl.Element(tn), pl.Element(K)),
                               lambda j, i: (pl.multiple_of(col0 + j * tn, 8), 0))],
        out_specs=pl.BlockSpec((tm, tn), lambda j, i: (i, j)),
        out_shape=jax.ShapeDtypeStruct((T, n_cols), BF16),
        scratch_shapes=[pltpu.VMEM((tn, K), BF16)],
        compiler_params=_cparams(("arbitrary", "arbitrary")),
        name=name,
    )(xn, wt)


GLA_ROWS = 256


def _gla_kernel(q_ref, k_ref, v_ref, r_ref, glr_ref, wup_ref, bg_ref, ng_ref, o_ref, st_ref):
    @pl.when(pl.program_id(0) == 0)
    def _():
        st_ref[...] = jnp.zeros_like(st_ref)

    z = jnp.dot(glr_ref[...], wup_ref[...], preferred_element_type=F32,
                precision=lax.Precision.HIGHEST) + bg_ref[...]
    log_a = (jnp.minimum(z, 0.0) - jnp.log(1.0 + jnp.exp(-jnp.abs(z)))) * (1.0 / GLA_TAU)

    ri = lax.broadcasted_iota(I32, (CHUNK, CHUNK), 0)
    ci = lax.broadcasted_iota(I32, (CHUNK, CHUNK), 1)
    causal = ci <= ri
    tril = jnp.where(causal, 1.0, 0.0).astype(BF16)
    scale = GLA_DK ** -0.5
    ng = ng_ref[...]

    for c in range(GLA_ROWS // CHUNK):
        rows = slice(c * CHUNK, (c + 1) * CHUNK)
        la = log_a[rows]
        la_hi = la.astype(BF16)
        la_lo = (la - la_hi.astype(F32)).astype(BF16)
        bcum_all = (jnp.dot(tril, la_hi, preferred_element_type=F32)
                    + jnp.dot(tril, la_lo, preferred_element_type=F32))
        for h in range(GLA_HEADS):
            kcols = slice(h * GLA_DK, (h + 1) * GLA_DK)
            vcols = slice(h * GLA_DV, (h + 1) * GLA_DV)
            bcum = bcum_all[:, kcols]
            blast = bcum[CHUNK - 1:CHUNK, :]
            qc = q_ref[rows, kcols].astype(F32) * scale
            kc = k_ref[rows, kcols].astype(F32)
            q_dec = (qc * jnp.exp(bcum)).astype(BF16)
            k_inv = (kc * jnp.exp(-bcum)).astype(BF16)
            k_end = (kc * jnp.exp(blast - bcum)).astype(BF16)
            vv = v_ref[rows, vcols]
            a = lax.dot_general(q_dec, k_inv, (((1,), (1,)), ((), ())), preferred_element_type=F32)
            a = jnp.where(causal, a, 0.0).astype(BF16)
            st = st_ref[h]
            o = jnp.dot(a, vv, preferred_element_type=F32)
            o = o + lax.dot_general(q_dec, st.astype(BF16), (((1,), (1,)), ((), ())),
                                    preferred_element_type=F32)
            st_ref[h] = st * jnp.exp(blast) + lax.dot_general(
                vv, k_end, (((0,), (0,)), ((), ())), preferred_element_type=F32)
            ms = jnp.mean(o * o, axis=-1, keepdims=True)
            y = o * lax.rsqrt(ms + RMS_EPS) * ng
            rr = r_ref[rows, vcols].astype(F32)
            o_ref[rows, vcols] = (y * (rr * jax.nn.sigmoid(rr))).astype(o_ref.dtype)


def _gla(proj_a, glr, wup, bg, ng):
    T = proj_a.shape[0]
    R = GLA_ROWS
    return pl.pallas_call(
        _gla_kernel,
        grid=(T // R,),
        in_specs=[pl.BlockSpec((R, GLA_QK), lambda n: (n, 0)),
                  pl.BlockSpec((R, GLA_QK), lambda n: (n, 1)),
                  pl.BlockSpec((R, GLA_V), lambda n: (n, 1)),
                  pl.BlockSpec((R, GLA_V), lambda n: (n, 2)),
                  pl.BlockSpec((R, GLA_RANK), lambda n: (n, 0)),
                  pl.BlockSpec((GLA_RANK, GLA_QK), lambda n: (0, 0)),
                  pl.BlockSpec((1, GLA_QK), lambda n: (0, 0)),
                  pl.BlockSpec((1, GLA_DV), lambda n: (0, 0))],
        out_specs=pl.BlockSpec((R, GLA_V), lambda n: (n, 0)),
        out_shape=jax.ShapeDtypeStruct((T, GLA_V), BF16),
        scratch_shapes=[pltpu.VMEM((GLA_HEADS, GLA_DV, GLA_DK), F32)],
        compiler_params=_cparams(("arbitrary",)),
        name="gla",
    )(proj_a, proj_a, proj_a, proj_a, glr, wup, bg, ng)


ATT_QROWS = 256
ATT_PAD = ATT_LEFT * CHUNK
ATT_BAND = ATT_PAD + ATT_QROWS


ATT_EXT = 1024


def _att_bias_diagonals(rel_bias):
    m = np.arange(ATT_EXT)
    m = np.where(m < ATT_BAND, m, m - ATT_EXT)
    rel = np.clip(ATT_PAD - m, -REL_CLIP, REL_CLIP) + REL_CLIP
    return rel_bias.astype(F32)[:, rel].reshape(ATT_HEADS, 1, ATT_EXT)


def _att_kernel(q_ref, k_ref, v_ref, qg_ref, kg_ref, ext_ref, o_ref, kn_ref, vp_ref, tab_ref):
    qb = pl.program_id(1)

    @pl.when(qb == 0)
    def _():
        ext = jnp.broadcast_to(ext_ref[...], (ATT_QROWS, ATT_EXT))
        tab = pltpu.roll(ext, 0, 1, stride=1, stride_axis=0)[:, :ATT_BAND]
        qc = lax.broadcasted_iota(I32, (ATT_QROWS, ATT_BAND), 0) // CHUNK
        kc = lax.broadcasted_iota(I32, (ATT_QROWS, ATT_BAND), 1) // CHUNK
        tab_ref[...] = jnp.where((kc >= qc) & (kc <= qc + ATT_LEFT), tab, NEG_BIG)
        kk = k_ref[...].astype(F32)
        ms = jnp.mean(kk * kk, axis=-1, keepdims=True)
        kn = kk * lax.rsqrt(ms + RMS_EPS) * kg_ref[...]
        kn_ref[0:ATT_PAD, :] = jnp.zeros((ATT_PAD, ATT_HD), BF16)
        kn_ref[ATT_PAD:, :] = kn.astype(BF16)
        vp_ref[0:ATT_PAD, :] = jnp.zeros((ATT_PAD, ATT_HD), BF16)
        vp_ref[ATT_PAD:, :] = v_ref[...]

    q = q_ref[...].astype(F32)
    ms = jnp.mean(q * q, axis=-1, keepdims=True)
    qn = (q * lax.rsqrt(ms + RMS_EPS) * qg_ref[...] * (ATT_HD ** -0.5)).astype(BF16)
    start = pl.multiple_of(qb * ATT_QROWS, ATT_QROWS)
    kb = kn_ref[pl.ds(start, ATT_BAND), :]
    vb = vp_ref[pl.ds(start, ATT_BAND), :]
    s = lax.dot_general(qn, kb, (((1,), (1,)), ((), ())), preferred_element_type=F32)
    s = s + tab_ref[...]
    kpos = start - ATT_PAD + lax.broadcasted_iota(I32, (1, ATT_BAND), 1)
    s = jnp.where(kpos >= 0, s, NEG_BIG)
    m = jnp.max(s, axis=-1, keepdims=True)
    p = jnp.exp(s - m)
    l = jnp.sum(p, axis=-1, keepdims=True)
    o = jnp.dot(p.astype(BF16), vb, preferred_element_type=F32)
    o_ref[...] = (o / l).astype(o_ref.dtype)


def _attention(proj_b, qg, kg, ext):
    T = proj_b.shape[0]
    R = ATT_QROWS
    return pl.pallas_call(
        _att_kernel,
        grid=(ATT_HEADS, T // R),
        in_specs=[pl.BlockSpec((R, ATT_HD), lambda h, n: (n, h)),
                  pl.BlockSpec((T, ATT_HD), lambda h, n: (0, ATT_HEADS + h)),
                  pl.BlockSpec((T, ATT_HD), lambda h, n: (0, 2 * ATT_HEADS + h)),
                  pl.BlockSpec((1, ATT_HD), lambda h, n: (0, 0)),
                  pl.BlockSpec((1, ATT_HD), lambda h, n: (0, 0)),
                  pl.BlockSpec((None, 1, ATT_EXT), lambda h, n: (h, 0, 0))],
        out_specs=pl.BlockSpec((R, ATT_HD), lambda h, n: (n, h)),
        out_shape=jax.ShapeDtypeStruct((T, ATT_W), BF16),
        scratch_shapes=[pltpu.VMEM((T + ATT_PAD, ATT_HD), BF16),
                        pltpu.VMEM((T + ATT_PAD, ATT_HD), BF16),
                        pltpu.VMEM((R, ATT_BAND), F32)],
        compiler_params=_cparams(("arbitrary", "arbitrary")),
        name="chunk_attention",
    )(proj_b, proj_b, proj_b, qg, kg, ext)


def _merge_kernel(yg_ref, ya_ref, wg_ref, wa_ref, gg_ref, ga_ref, bb_ref, o_ref):
    pg = jnp.dot(yg_ref[...], wg_ref[...], preferred_element_type=F32)
    pa = jnp.dot(ya_ref[...], wa_ref[...], preferred_element_type=F32)
    bb = bb_ref[...]
    sg = jax.nn.sigmoid(gg_ref[...].astype(F32) + bb[0:1, :])
    sa = jax.nn.sigmoid(ga_ref[...].astype(F32) + bb[1:2, :])
    o_ref[...] = (sg * pg + sa * pa).astype(o_ref.dtype)


def _merge(y_gla, y_att, wg, wa, proj_b, bb):
    T = y_gla.shape[0]
    tm, tn = 1024, 1024
    gate0 = 3 * ATT_W // tn
    return pl.pallas_call(
        _merge_kernel,
        grid=(D_MODEL // tn, T // tm),
        in_specs=[pl.BlockSpec((tm, GLA_V), lambda j, i: (i, 0)),
                  pl.BlockSpec((tm, ATT_W), lambda j, i: (i, 0)),
                  pl.BlockSpec((GLA_V, tn), lambda j, i: (0, j)),
                  pl.BlockSpec((ATT_W, tn), lambda j, i: (0, j)),
                  pl.BlockSpec((tm, tn), lambda j, i: (i, gate0 + j)),
                  pl.BlockSpec((tm, tn), lambda j, i: (i, gate0 + D_MODEL // tn + j)),
                  pl.BlockSpec((2, tn), lambda j, i: (0, j))],
        out_specs=pl.BlockSpec((tm, tn), lambda j, i: (i, j)),
        out_shape=jax.ShapeDtypeStruct((T, D_MODEL), BF16),
        compiler_params=_cparams(("arbitrary", "arbitrary")),
        name="merge",
    )(y_gla, y_att, wg, wa, proj_b, proj_b, bb)


ROUTE_ROWS = 256
ROUTE_LANES = 128


def _route_kernel(m_ref, x_ref, wo_ref, g2_ref, wr_ref, br_ref,
                  h1_ref, hn_ref, idx_ref, wt_ref, rank_ref, cnt_ref, run_ref):
    @pl.when(pl.program_id(0) == 0)
    def _():
        run_ref[...] = jnp.zeros_like(run_ref)

    h1 = x_ref[...] + jnp.dot(m_ref[...], wo_ref[...], preferred_element_type=F32)
    h1_ref[...] = h1
    ms = jnp.mean(h1 * h1, axis=-1, keepdims=True)
    hn = h1 * lax.rsqrt(ms + RMS_EPS) * g2_ref[...]
    hn_ref[...] = hn
    wr = wr_ref[...]
    hn_hi = hn.astype(BF16)
    hn_lo = (hn - hn_hi.astype(F32)).astype(BF16)
    wr_hi = wr.astype(BF16)
    wr_lo = (wr - wr_hi.astype(F32)).astype(BF16)
    logits = (jnp.dot(hn_hi, wr_hi, preferred_element_type=F32)
              + jnp.dot(hn_hi, wr_lo, preferred_element_type=F32)
              + jnp.dot(hn_lo, wr_hi, preferred_element_type=F32)) + br_ref[...]

    R = ROUTE_ROWS
    lanes = lax.broadcasted_iota(I32, (R, N_EXPERTS), 1)
    work = logits
    vals, sels, idxs = [], [], []
    for _ in range(TOP_K):
        m = jnp.max(work, axis=-1, keepdims=True)
        idx = jnp.min(jnp.where(work == m, lanes, N_EXPERTS), axis=-1, keepdims=True)
        sel = lanes == idx
        vals.append(m)
        idxs.append(idx)
        sels.append(sel)
        work = jnp.where(sel, -jnp.inf, work)
    es = [jnp.exp(v - vals[0]) for v in vals]
    denom = es[0] + es[1] + es[2] + es[3]

    hot = jnp.zeros((R, N_EXPERTS), F32)
    for sel in sels:
        hot = jnp.where(sel, 1.0, hot)
    ri = lax.broadcasted_iota(I32, (R, R), 0)
    ci = lax.broadcasted_iota(I32, (R, R), 1)
    strict = jnp.where(ci < ri, 1.0, 0.0).astype(BF16)
    before = jnp.dot(strict, hot.astype(BF16), preferred_element_type=F32) + run_ref[...]
    run_new = run_ref[...] + jnp.sum(hot, axis=0, keepdims=True)
    run_ref[...] = run_new
    cnt_ref[...] = run_new

    ol = lax.broadcasted_iota(I32, (R, ROUTE_LANES), 1)
    idx_o = jnp.zeros((R, ROUTE_LANES), I32)
    wt_o = jnp.zeros((R, ROUTE_LANES), F32)
    rank_o = jnp.zeros((R, ROUTE_LANES), I32)
    for k in range(TOP_K):
        rk = jnp.sum(jnp.where(sels[k], before, 0.0), axis=-1, keepdims=True).astype(I32)
        idx_o = jnp.where(ol == k, idxs[k], idx_o)
        wt_o = jnp.where(ol == k, es[k] / denom, wt_o)
        rank_o = jnp.where(ol == k, rk, rank_o)
    idx_ref[...] = idx_o
    wt_ref[...] = wt_o
    rank_ref[...] = rank_o


def _route(merged, x2, wo, g2, wr, br):
    T = x2.shape[0]
    R = ROUTE_ROWS
    row = lambda i: (i, 0)
    fixed = lambda i: (0, 0)
    return pl.pallas_call(
        _route_kernel,
        grid=(T // R,),
        in_specs=[pl.BlockSpec((R, D_MODEL), row),
                  pl.BlockSpec((R, D_MODEL), row),
                  pl.BlockSpec((D_MODEL, D_MODEL), fixed),
                  pl.BlockSpec((1, D_MODEL), fixed),
                  pl.BlockSpec((D_MODEL, N_EXPERTS), fixed),
                  pl.BlockSpec((1, N_EXPERTS), fixed)],
        out_specs=[pl.BlockSpec((R, D_MODEL), row),
                   pl.BlockSpec((R, D_MODEL), row),
                   pl.BlockSpec((R, ROUTE_LANES), row),
                   pl.BlockSpec((R, ROUTE_LANES), row),
                   pl.BlockSpec((R, ROUTE_LANES), row),
                   pl.BlockSpec((1, N_EXPERTS), fixed)],
        out_shape=[jax.ShapeDtypeStruct((T, D_MODEL), F32),
                   jax.ShapeDtypeStruct((T, D_MODEL), F32),
                   jax.ShapeDtypeStruct((T, ROUTE_LANES), I32),
                   jax.ShapeDtypeStruct((T, ROUTE_LANES), F32),
                   jax.ShapeDtypeStruct((T, ROUTE_LANES), I32),
                   jax.ShapeDtypeStruct((1, N_EXPERTS), F32)],
        scratch_shapes=[pltpu.VMEM((1, N_EXPERTS), F32)],
        compiler_params=_cparams(("arbitrary",)),
        name="outproj_route",
    )(merged, x2, wo, g2, wr, br)


DISP_ROWS = 512
DMA_UNROLL = 8


def _dispatch_kernel(pos_ref, hn_ref, xs_ref, sem):
    def row_copy(t, k):
        dst = pos_ref[0, 0, t * TOP_K + k]
        return pltpu.make_async_copy(hn_ref.at[pl.ds(t, 1), :], xs_ref.at[pl.ds(dst, 1), :], sem)

    def issue(i, carry):
        for tt in range(DMA_UNROLL):
            for k in range(TOP_K):
                row_copy(i * DMA_UNROLL + tt, k).start(priority=k % 2)
        return carry

    lax.fori_loop(0, DISP_ROWS // DMA_UNROLL, issue, 0)

    def drain(i, carry):
        for tt in range(DMA_UNROLL):
            for k in range(TOP_K):
                row_copy(i * DMA_UNROLL + tt, k).wait()
        return carry

    lax.fori_loop(0, DISP_ROWS // DMA_UNROLL, drain, 0)


def _dispatch(hn, pos, n_alloc):
    T = hn.shape[0]
    R = DISP_ROWS
    pos3 = pos.reshape(T // R, 1, R * TOP_K)
    return pl.pallas_call(
        _dispatch_kernel,
        grid=(T // R,),
        in_specs=[pl.BlockSpec((1, 1, R * TOP_K), lambda i: (i, 0, 0), memory_space=pltpu.SMEM),
                  pl.BlockSpec((R, D_MODEL), lambda i: (i, 0))],
        out_specs=pl.BlockSpec(memory_space=pl.ANY),
        out_shape=jax.ShapeDtypeStruct((n_alloc, D_MODEL), F32),
        scratch_shapes=[pltpu.SemaphoreType.DMA(())],
        compiler_params=_cparams(("arbitrary",)),
        name="dispatch",
    )(pos3, hn)


def _expert_kernel(ue_ref, ur_ref, un_ref, xs_ref, wg_ref, wu_ref, wd_ref, bg_ref, bu_ref, bd_ref,
                   ys_ref, xb_ref, h_ref, stage_ref, ybuf_ref, pend_ref, xsem, ysem):
    u = pl.program_id(0)
    s = pl.program_id(1)
    nu = pl.num_programs(0)
    nblk = un_ref[u]
    row0 = ur_ref[u]
    slot = u % 2

    def blk_rows(b):
        return pl.ds(pl.multiple_of(b * ROW_BLK, ROW_BLK), ROW_BLK)

    def for_row_blocks(fn):
        def pair(i, carry):
            fn(pl.ds(pl.multiple_of(i * (2 * ROW_BLK), 2 * ROW_BLK), 2 * ROW_BLK))
            return carry

        lax.fori_loop(0, lax.shift_right_logical(nblk, 1), pair, 0)

        @pl.when((nblk & 1) == 1)
        def _():
            fn(blk_rows(nblk - 1))

    def x_copy(unit, b):
        rows = pl.ds(pl.multiple_of(ur_ref[unit] + b * ROW_BLK, ROW_BLK), ROW_BLK)
        return pltpu.make_async_copy(xs_ref.at[rows, :], stage_ref, xsem)

    def y_copy(b, n, ys):
        rows = pl.ds(pl.multiple_of(row0 + b * ROW_BLK, ROW_BLK), ROW_BLK)
        cols = pl.ds(pl.multiple_of(n * FF_TILE, FF_TILE), FF_TILE)
        return pltpu.make_async_copy(ybuf_ref.at[ys, blk_rows(b), :], ys_ref.at[rows, cols], ysem)

    def drain_stores():
        def wait_one(i, carry):
            y_copy(0, 0, 0).wait()
            return carry

        lax.fori_loop(0, pend_ref[0], wait_one, 0)
        pend_ref[0] = 0

    @pl.when((u == 0) & (s == 0))
    def _():
        pend_ref[0] = 0

        def load(b, carry):
            cp = x_copy(0, b)
            cp.start()
            cp.wait()
            xb_ref[0, blk_rows(b), :] = stage_ref[...].astype(BF16)
            return carry

        lax.fori_loop(0, nblk, load, 0)

    nxt = jnp.minimum(u + 1, nu - 1)
    prefetch = (u + 1 < nu) & (s < un_ref[nxt])

    @pl.when(prefetch)
    def _():
        x_copy(nxt, s).start()

    @pl.when(s < NJ)
    def _():
        wg = wg_ref[...].astype(BF16)
        wu = wu_ref[...].astype(BF16)
        bg = bg_ref[...]
        bu = bu_ref[...]

        def up_rows(rows):
            xb = xb_ref[slot, rows, :]
            g = jnp.minimum(jnp.dot(xb, wg, preferred_element_type=F32) + bg, SWIGLU_LIMIT)
            up = jnp.clip(jnp.dot(xb, wu, preferred_element_type=F32) + bu,
                          -SWIGLU_LIMIT, SWIGLU_LIMIT)
            h_ref[s, rows, :] = ((up + 1.0) * (g * jax.nn.sigmoid(SWIGLU_ALPHA * g))).astype(BF16)

        for_row_blocks(up_rows)

    @pl.when(s >= NJ)
    def _():
        n = s - NJ
        ys = n % 2
        wd = wd_ref[...].astype(BF16)
        bd = bd_ref[...]

        def down_rows(rows):
            hb = jnp.concatenate([h_ref[t, rows, :] for t in range(NJ)], axis=1)
            ybuf_ref[ys, rows, :] = jnp.dot(hb, wd, preferred_element_type=F32) + bd

        for_row_blocks(down_rows)
        drain_stores()

        def issue(b, carry):
            y_copy(b, n, ys).start()
            return carry

        lax.fori_loop(0, nblk, issue, 0)
        pend_ref[0] = nblk

    @pl.when(prefetch)
    def _():
        x_copy(nxt, s).wait()
        xb_ref[1 - slot, blk_rows(s), :] = stage_ref[...].astype(BF16)

    @pl.when((u == nu - 1) & (s == 2 * NJ - 1))
    def _():
        drain_stores()


def _experts(xs, unit_e, unit_row0, unit_nblk, w_gate, w_up, w_down, b_gate, b_up, b_down):
    n_alloc = xs.shape[0]
    n_units = unit_e.shape[0]
    last = NJ - 1

    def up_tile(u, s, ue, ur, un):
        return (ue[u], 0, jnp.where(un[u] > 0, jnp.minimum(s, last), last))

    def down_tile(u, s, ue, ur, un):
        return (ue[u], 0, jnp.where(un[u] > 0, jnp.maximum(s - NJ, 0), last))

    grid_spec = pltpu.PrefetchScalarGridSpec(
        num_scalar_prefetch=3,
        grid=(n_units, 2 * NJ),
        in_specs=[
            pl.BlockSpec(memory_space=pl.ANY),
            pl.BlockSpec((None, D_MODEL, FF_TILE), up_tile),
            pl.BlockSpec((None, D_MODEL, FF_TILE), up_tile),
            pl.BlockSpec((None, D_FF, FF_TILE), down_tile),
            pl.BlockSpec((None, 1, FF_TILE), up_tile),
            pl.BlockSpec((None, 1, FF_TILE), up_tile),
            pl.BlockSpec((None, 1, FF_TILE), down_tile),
        ],
        out_specs=pl.BlockSpec(memory_space=pl.ANY),
        scratch_shapes=[pltpu.VMEM((2, SEG_ROWS, D_MODEL), BF16),
                        pltpu.VMEM((NJ, SEG_ROWS, FF_TILE), BF16),
                        pltpu.VMEM((ROW_BLK, D_MODEL), F32),
                        pltpu.VMEM((2, SEG_ROWS, FF_TILE), F32),
                        pltpu.SMEM((1,), I32),
                        pltpu.SemaphoreType.DMA(()),
                        pltpu.SemaphoreType.DMA(())],
    )
    return pl.pallas_call(
        _expert_kernel,
        grid_spec=grid_spec,
        out_shape=jax.ShapeDtypeStruct((n_alloc, D_MODEL), F32),
        compiler_params=_cparams(("arbitrary", "arbitrary")),
        name="experts",
    )(unit_e, unit_row0, unit_nblk, xs, w_gate, w_up, w_down,
      b_gate.reshape(N_EXPERTS, 1, D_FF), b_up.reshape(N_EXPERTS, 1, D_FF),
      b_down.reshape(N_EXPERTS, 1, D_MODEL))


COMB_ROWS = 128


def _combine_kernel(pos_ref, nxt_ref, h1_ref, wt_ref, ys_ref, o_ref, buf_ref, sems):
    i = pl.program_id(0)
    n = pl.num_programs(0)
    slot = i % 2

    def row_copy(p_ref, s, t, k):
        src = p_ref[0, 0, t * TOP_K + k]
        return pltpu.make_async_copy(ys_ref.at[pl.ds(src, 1), :],
                                     buf_ref.at[s, k, pl.ds(t, 1), :], sems.at[s])

    def issue_all(p_ref, s):
        def issue(j, carry):
            for tt in range(DMA_UNROLL):
                for k in range(TOP_K):
                    row_copy(p_ref, s, j * DMA_UNROLL + tt, k).start(priority=k % 2)
            return carry

        lax.fori_loop(0, COMB_ROWS // DMA_UNROLL, issue, 0)

    @pl.when(i == 0)
    def _():
        issue_all(pos_ref, 0)

    @pl.when(i + 1 < n)
    def _():
        issue_all(nxt_ref, 1 - slot)

    def drain(j, carry):
        for tt in range(DMA_UNROLL):
            for k in range(TOP_K):
                row_copy(pos_ref, slot, j * DMA_UNROLL + tt, k).wait()
        return carry

    lax.fori_loop(0, COMB_ROWS // DMA_UNROLL, drain, 0)

    wt = wt_ref[...]
    acc = h1_ref[...]
    for k in range(TOP_K):
        acc = acc + wt[:, k:k + 1] * buf_ref[slot, k]
    o_ref[...] = acc


def _combine(h1, wt, ys, pos):
    T = h1.shape[0]
    R = COMB_ROWS
    nsteps = T // R
    pos3 = pos.reshape(nsteps, 1, R * TOP_K)
    return pl.pallas_call(
        _combine_kernel,
        grid=(nsteps,),
        in_specs=[pl.BlockSpec((1, 1, R * TOP_K), lambda i: (i, 0, 0), memory_space=pltpu.SMEM),
                  pl.BlockSpec((1, 1, R * TOP_K), lambda i: (jnp.minimum(i + 1, nsteps - 1), 0, 0),
                               memory_space=pltpu.SMEM),
                  pl.BlockSpec((R, D_MODEL), lambda i: (i, 0)),
                  pl.BlockSpec((R, ROUTE_LANES), lambda i: (i, 0)),
                  pl.BlockSpec(memory_space=pl.ANY)],
        out_specs=pl.BlockSpec((R, D_MODEL), lambda i: (i, 0)),
        out_shape=jax.ShapeDtypeStruct((T, D_MODEL), F32),
        scratch_shapes=[pltpu.VMEM((2, TOP_K, R, D_MODEL), F32),
                        pltpu.SemaphoreType.DMA((2,))],
        compiler_params=_cparams(("arbitrary",)),
        name="combine",
    )(pos3, pos3, h1, wt, ys)


def _max_units(T):
    max_blocks = (T * TOP_K) // ROW_BLK + N_EXPERTS
    return (max_blocks + N_EXPERTS * (SEG_BLKS - 1)) // SEG_BLKS


def _routing_tables(counts, idx, rank, T):
    counts = counts.astype(I32)
    nblk = (counts + ROW_BLK - 1) // ROW_BLK
    pstart = (jnp.cumsum(nblk) - nblk) * ROW_BLK
    hot = idx[..., None] == jnp.arange(N_EXPERTS, dtype=I32)
    pos = jnp.sum(jnp.where(hot, pstart, 0), axis=-1) + rank
    units = (nblk + SEG_BLKS - 1) // SEG_BLKS
    uend = jnp.cumsum(units)
    ustart = uend - units
    n_units = _max_units(T)
    u = jnp.arange(n_units, dtype=I32)
    total = uend[-1]
    ue = jnp.minimum(jnp.searchsorted(uend, u, side="right").astype(I32), N_EXPERTS - 1)
    local = u - jnp.take(ustart, ue)
    un = jnp.clip(jnp.take(nblk, ue) - local * SEG_BLKS, 0, SEG_BLKS)
    ur = jnp.take(pstart, ue) + local * SEG_ROWS
    live = u < total
    last_e = jnp.take(ue, jnp.maximum(total - 1, 0))
    ue = jnp.where(live, ue, last_e)
    un = jnp.where(live, un, 0)
    ur = jnp.where(live, ur, 0)
    return pos.astype(I32), ue, ur.astype(I32), un.astype(I32)


def kernel(x, norm1_g, w_in, w_gla_gate_up, b_gla_gate, gla_norm_g, q_norm_g, k_norm_g, rel_bias,
           b_branch_gate, w_branch_gla, w_branch_att, w_out, norm2_g, w_router, b_router,
           w_gate, b_gate, w_up, b_up, w_down, b_down):
    B, S, D = x.shape
    T = B * S
    h = x.reshape(T, D)
    wt_in = jnp.swapaxes(w_in, 1, 2)
    for l in range(w_in.shape[0]):
        xn, glr = _norm1(h, norm1_g[l].reshape(1, D), wt_in[l, COL_LR:COL_B, :].astype(BF16))
        proj_a = _proj(xn, wt_in[l], 0, COLS_A, "inproj_gla")
        proj_b = _proj(xn, wt_in[l], COL_B, COLS_B, "inproj_att")
        y_gla = _gla(proj_a, glr, w_gla_gate_up[l], b_gla_gate[l].reshape(1, GLA_QK),
                     gla_norm_g[l].reshape(1, GLA_DV))
        y_att = _attention(proj_b, q_norm_g[l].reshape(1, ATT_HD), k_norm_g[l].reshape(1, ATT_HD),
                           _att_bias_diagonals(rel_bias[l]))
        merged = _merge(y_gla, y_att, w_branch_gla[l].astype(BF16), w_branch_att[l].astype(BF16),
                        proj_b, b_branch_gate[l])
        h1, hn, idx, wt, rank, counts = _route(
            merged, h, w_out[l].astype(BF16), norm2_g[l].reshape(1, D), w_router[l],
            b_router[l].reshape(1, N_EXPERTS))
        pos, ue, ur, un = _routing_tables(counts[0], idx[:, :TOP_K], rank[:, :TOP_K], T)
        n_alloc = ((T * TOP_K) // ROW_BLK + N_EXPERTS) * ROW_BLK
        xs = _dispatch(hn, pos.reshape(-1), n_alloc)
        ys = _experts(xs, ue, ur, un, w_gate[l], w_up[l], w_down[l], b_gate[l], b_up[l], b_down[l])
        h = _combine(h1, wt, ys, pos.reshape(-1))
    return h.reshape(B, S, D)
```

```python
import functools

import jax
import jax.numpy as jnp
import numpy as np
from jax import lax
from jax.experimental import pallas as pl
from jax.experimental.pallas import tpu as pltpu

F32 = jnp.float32
BF16 = jnp.bfloat16
I32 = jnp.int32

D_MODEL = 2048
CHUNK = 64
RMS_EPS = 1e-6
GLA_HEADS = 4
GLA_DK = 256
GLA_DV = 512
GLA_RANK = 16
GLA_TAU = 16.0
GLA_QK = GLA_HEADS * GLA_DK
GLA_V = GLA_HEADS * GLA_DV
ATT_HEADS = 8
ATT_HD = 128
ATT_W = ATT_HEADS * ATT_HD
ATT_LEFT = 8
REL_CLIP = 128
N_EXPERTS = 32
TOP_K = 4
D_FF = 2048
SWIGLU_LIMIT = 7.0
SWIGLU_ALPHA = 1.702

COLS_A = 2 * GLA_QK + 2 * GLA_V
COL_LR = COLS_A
COL_B = COLS_A + GLA_RANK
COLS_B = 3 * ATT_W + 2 * D_MODEL

VMEM_LIMIT = 56 * 1024 * 1024
NEG_BIG = -1e30

ROW_BLK = 256
SEG_BLKS = 5
SEG_ROWS = ROW_BLK * SEG_BLKS
FF_TILE = 512
NJ = D_FF // FF_TILE
assert D_MODEL // FF_TILE == NJ and SEG_BLKS <= 2 * NJ


def _cparams(sem):
    return pltpu.CompilerParams(dimension_semantics=sem, vmem_limit_bytes=VMEM_LIMIT)


def _norm1_kernel(x_ref, g_ref, wlr_ref, xn_ref, glr_ref):
    x = x_ref[...]
    ms = jnp.mean(x * x, axis=-1, keepdims=True)
    y = (x * lax.rsqrt(ms + RMS_EPS) * g_ref[...]).astype(BF16)
    xn_ref[...] = y
    glr_ref[...] = lax.dot_general(y, wlr_ref[...], (((1,), (1,)), ((), ())),
                                   preferred_element_type=F32)


def _norm1(x2, g, wlr):
    T = x2.shape[0]
    tm = 512
    return pl.pallas_call(
        _norm1_kernel,
        grid=(T // tm,),
        in_specs=[pl.BlockSpec((tm, D_MODEL), lambda i: (i, 0)),
                  pl.BlockSpec((1, D_MODEL), lambda i: (0, 0)),
                  pl.BlockSpec((GLA_RANK, D_MODEL), lambda i: (0, 0))],
        out_specs=[pl.BlockSpec((tm, D_MODEL), lambda i: (i, 0)),
                   pl.BlockSpec((tm, GLA_RANK), lambda i: (i, 0))],
        out_shape=[jax.ShapeDtypeStruct((T, D_MODEL), BF16),
                   jax.ShapeDtypeStruct((T, GLA_RANK), F32)],
        compiler_params=_cparams(("parallel",)),
        name="norm1",
    )(x2, g, wlr)


def _proj_kernel(x_ref, w_ref, o_ref, wb_ref):
    @pl.when(pl.program_id(1) == 0)
    def _():
        wb_ref[...] = w_ref[...].astype(BF16)

    o_ref[...] = lax.dot_general(x_ref[...], wb_ref[...], (((1,), (1,)), ((), ())),
                                 preferred_element_type=F32).astype(o_ref.dtype)


def _proj(xn, wt, col0, n_cols, name):
    T, K = xn.shape
    tm, tn = 1024, 1024
    assert col0 % 8 == 0
    return pl.pallas_call(
        _proj_kernel,
        grid=(n_cols // tn, T // tm),
        in_specs=[pl.BlockSpec((tm, K), lambda j, i: (i, 0)),
                  pl.BlockSpec((pl.Element(tn), pl.Element(K)),
                               lambda j, i: (pl.multiple_of(col0 + j * tn, 8), 0))],
        out_specs=pl.BlockSpec((tm, tn), lambda j, i: (i, j)),
        out_shape=jax.ShapeDtypeStruct((T, n_cols), BF16),
        scratch_shapes=[pltpu.VMEM((tn, K), BF16)],
        compiler_params=_cparams(("arbitrary", "arbitrary")),
        name=name,
    )(xn, wt)


GLA_ROWS = 256


def _gla_kernel(q_ref, k_ref, v_ref, r_ref, glr_ref, wup_ref, bg_ref, ng_ref, o_ref, st_ref):
    @pl.when(pl.program_id(0) == 0)
    def _():
        st_ref[...] = jnp.zeros_like(st_ref)

    z = jnp.dot(glr_ref[...], wup_ref[...], preferred_element_type=F32,
                precision=lax.Precision.HIGHEST) + bg_ref[...]
    log_a = (jnp.minimum(z, 0.0) - jnp.log(1.0 + jnp.exp(-jnp.abs(z)))) * (1.0 / GLA_TAU)

    ri = lax.broadcasted_iota(I32, (CHUNK, CHUNK), 0)
    ci = lax.broadcasted_iota(I32, (CHUNK, CHUNK), 1)
    causal = ci <= ri
    tril = jnp.where(causal, 1.0, 0.0).astype(BF16)
    scale = GLA_DK ** -0.5
    ng = ng_ref[...]

    for c in range(GLA_ROWS // CHUNK):
        rows = slice(c * CHUNK, (c + 1) * CHUNK)
        la = log_a[rows]
        la_hi = la.astype(BF16)
        la_lo = (la - la_hi.astype(F32)).astype(BF16)
        bcum_all = (jnp.dot(tril, la_hi, preferred_element_type=F32)
                    + jnp.dot(tril, la_lo, preferred_element_type=F32))
        for h in range(GLA_HEADS):
            kcols = slice(h * GLA_DK, (h + 1) * GLA_DK)
            vcols = slice(h * GLA_DV, (h + 1) * GLA_DV)
            bcum = bcum_all[:, kcols]
            blast = bcum[CHUNK - 1:CHUNK, :]
            qc = q_ref[rows, kcols].astype(F32) * scale
            kc = k_ref[rows, kcols].astype(F32)
            q_dec = (qc * jnp.exp(bcum)).astype(BF16)
            k_inv = (kc * jnp.exp(-bcum)).astype(BF16)
            k_end = (kc * jnp.exp(blast - bcum)).astype(BF16)
            vv = v_ref[rows, vcols]
            a = lax.dot_general(q_dec, k_inv, (((1,), (1,)), ((), ())), preferred_element_type=F32)
            a = jnp.where(causal, a, 0.0).astype(BF16)
            st = st_ref[h]
            o = jnp.dot(a, vv, preferred_element_type=F32)
            o = o + lax.dot_general(q_dec, st.astype(BF16), (((1,), (1,)), ((), ())),
                                    preferred_element_type=F32)
            st_ref[h] = st * jnp.exp(blast) + lax.dot_general(
                vv, k_end, (((0,), (0,)), ((), ())), preferred_element_type=F32)
            ms = jnp.mean(o * o, axis=-1, keepdims=True)
            y = o * lax.rsqrt(ms + RMS_EPS) * ng
            rr = r_ref[rows, vcols].astype(F32)
            o_ref[rows, vcols] = (y * (rr * jax.nn.sigmoid(rr))).astype(o_ref.dtype)


def _gla(proj_a, glr, wup, bg, ng):
    T = proj_a.shape[0]
    R = GLA_ROWS
    return pl.pallas_call(
        _gla_kernel,
        grid=(T // R,),
        in_specs=[pl.BlockSpec((R, GLA_QK), lambda n: (n, 0)),
                  pl.BlockSpec((R, GLA_QK), lambda n: (n, 1)),
                  pl.BlockSpec((R, GLA_V), lambda n: (n, 1)),
                  pl.BlockSpec((R, GLA_V), lambda n: (n, 2)),
                  pl.BlockSpec((R, GLA_RANK), lambda n: (n, 0)),
                  pl.BlockSpec((GLA_RANK, GLA_QK), lambda n: (0, 0)),
                  pl.BlockSpec((1, GLA_QK), lambda n: (0, 0)),
                  pl.BlockSpec((1, GLA_DV), lambda n: (0, 0))],
        out_specs=pl.BlockSpec((R, GLA_V), lambda n: (n, 0)),
        out_shape=jax.ShapeDtypeStruct((T, GLA_V), BF16),
        scratch_shapes=[pltpu.VMEM((GLA_HEADS, GLA_DV, GLA_DK), F32)],
        compiler_params=_cparams(("arbitrary",)),
        name="gla",
    )(proj_a, proj_a, proj_a, proj_a, glr, wup, bg, ng)


ATT_QROWS = 256
ATT_PAD = ATT_LEFT * CHUNK
ATT_BAND = ATT_PAD + ATT_QROWS


ATT_EXT = 1024
ATT_GROUP = 2


def _att_bias_diagonals(rel_bias):
    m = np.arange(ATT_EXT)
    m = np.where(m < ATT_BAND, m, m - ATT_EXT)
    rel = np.clip(ATT_PAD - m, -REL_CLIP, REL_CLIP) + REL_CLIP
    return rel_bias.astype(F32)[:, rel].reshape(ATT_HEADS, 1, ATT_EXT)


def _att_kernel(q_ref, k_ref, v_ref, qg_ref, kg_ref, ext_ref, o_ref, kn_ref, vp_ref, tab_ref):
    qb = pl.program_id(1)

    @pl.when(qb == 0)
    def _():
        qc = lax.broadcasted_iota(I32, (ATT_QROWS, ATT_BAND), 0) // CHUNK
        kc = lax.broadcasted_iota(I32, (ATT_QROWS, ATT_BAND), 1) // CHUNK
        in_band = (kc >= qc) & (kc <= qc + ATT_LEFT)
        for g in range(ATT_GROUP):
            cols = slice(g * ATT_HD, (g + 1) * ATT_HD)
            ext = jnp.broadcast_to(ext_ref[g], (ATT_QROWS, ATT_EXT))
            tab = pltpu.roll(ext, 0, 1, stride=1, stride_axis=0)[:, :ATT_BAND]
            tab_ref[g] = jnp.where(in_band, tab, NEG_BIG)
            kk = k_ref[:, cols].astype(F32)
            ms = jnp.mean(kk * kk, axis=-1, keepdims=True)
            kn = kk * lax.rsqrt(ms + RMS_EPS) * kg_ref[...]
            kn_ref[g, 0:ATT_PAD, :] = jnp.zeros((ATT_PAD, ATT_HD), BF16)
            kn_ref[g, ATT_PAD:, :] = kn.astype(BF16)
            vp_ref[g, 0:ATT_PAD, :] = jnp.zeros((ATT_PAD, ATT_HD), BF16)
            vp_ref[g, ATT_PAD:, :] = v_ref[:, cols]

    start = pl.multiple_of(qb * ATT_QROWS, ATT_QROWS)
    kpos = start - ATT_PAD + lax.broadcasted_iota(I32, (1, ATT_BAND), 1)
    for g in range(ATT_GROUP):
        cols = slice(g * ATT_HD, (g + 1) * ATT_HD)
        q = q_ref[:, cols].astype(F32)
        ms = jnp.mean(q * q, axis=-1, keepdims=True)
        qn = (q * lax.rsqrt(ms + RMS_EPS) * qg_ref[...] * (ATT_HD ** -0.5)).astype(BF16)
        kb = kn_ref[g, pl.ds(start, ATT_BAND), :]
        vb = vp_ref[g, pl.ds(start, ATT_BAND), :]
        s = lax.dot_general(qn, kb, (((1,), (1,)), ((), ())), preferred_element_type=F32)
        s = s + tab_ref[g]
        s = jnp.where(kpos >= 0, s, NEG_BIG)
        m = jnp.max(s, axis=-1, keepdims=True)
        p = jnp.exp(s - m)
        l = jnp.sum(p, axis=-1, keepdims=True)
        o = jnp.dot(p.astype(BF16), vb, preferred_element_type=F32)
        o_ref[:, cols] = (o / l).astype(o_ref.dtype)


def _attention(proj_b, qg, kg, ext):
    T = proj_b.shape[0]
    R = ATT_QROWS
    GW = ATT_GROUP * ATT_HD
    NG = ATT_HEADS // ATT_GROUP
    return pl.pallas_call(
        _att_kernel,
        grid=(NG, T // R),
        in_specs=[pl.BlockSpec((R, GW), lambda h, n: (n, h)),
                  pl.BlockSpec((T, GW), lambda h, n: (0, NG + h)),
                  pl.BlockSpec((T, GW), lambda h, n: (0, 2 * NG + h)),
                  pl.BlockSpec((1, ATT_HD), lambda h, n: (0, 0)),
                  pl.BlockSpec((1, ATT_HD), lambda h, n: (0, 0)),
                  pl.BlockSpec((ATT_GROUP, 1, ATT_EXT), lambda h, n: (h, 0, 0))],
        out_specs=pl.BlockSpec((R, GW), lambda h, n: (n, h)),
        out_shape=jax.ShapeDtypeStruct((T, ATT_W), BF16),
        scratch_shapes=[pltpu.VMEM((ATT_GROUP, T + ATT_PAD, ATT_HD), BF16),
                        pltpu.VMEM((ATT_GROUP, T + ATT_PAD, ATT_HD), BF16),
                        pltpu.VMEM((ATT_GROUP, R, ATT_BAND), F32)],
        compiler_params=_cparams(("arbitrary", "arbitrary")),
        name="chunk_attention",
    )(proj_b, proj_b, proj_b, qg, kg, ext)


def _merge_kernel(yg_ref, ya_ref, wg_ref, wa_ref, gg_ref, ga_ref, bb_ref, o_ref):
    pg = jnp.dot(yg_ref[...], wg_ref[...], preferred_element_type=F32)
    pa = jnp.dot(ya_ref[...], wa_ref[...], preferred_element_type=F32)
    bb = bb_ref[...]
    sg = jax.nn.sigmoid(gg_ref[...].astype(F32) + bb[0:1, :])
    sa = jax.nn.sigmoid(ga_ref[...].astype(F32) + bb[1:2, :])
    o_ref[...] = (sg * pg + sa * pa).astype(o_ref.dtype)


def _merge(y_gla, y_att, wg, wa, proj_b, bb):
    T = y_gla.shape[0]
    tm, tn = 1024, 1024
    gate0 = 3 * ATT_W // tn
    return pl.pallas_call(
        _merge_kernel,
        grid=(D_MODEL // tn, T // tm),
        in_specs=[pl.BlockSpec((tm, GLA_V), lambda j, i: (i, 0)),
                  pl.BlockSpec((tm, ATT_W), lambda j, i: (i, 0)),
                  pl.BlockSpec((GLA_V, tn), lambda j, i: (0, j)),
                  pl.BlockSpec((ATT_W, tn), lambda j, i: (0, j)),
                  pl.BlockSpec((tm, tn), lambda j, i: (i, gate0 + j)),
                  pl.BlockSpec((tm, tn), lambda j, i: (i, gate0 + D_MODEL // tn + j)),
                  pl.BlockSpec((2, tn), lambda j, i: (0, j))],
        out_specs=pl.BlockSpec((tm, tn), lambda j, i: (i, j)),
        out_shape=jax.ShapeDtypeStruct((T, D_MODEL), BF16),
        compiler_params=_cparams(("arbitrary", "arbitrary")),
        name="merge",
    )(y_gla, y_att, wg, wa, proj_b, proj_b, bb)


ROUTE_ROWS = 256
ROUTE_LANES = 128


def _route_kernel(m_ref, x_ref, wo_ref, g2_ref, wr_ref, br_ref,
                  h1_ref, hn_ref, idx_ref, wt_ref, rank_ref, cnt_ref, run_ref):
    @pl.when(pl.program_id(0) == 0)
    def _():
        run_ref[...] = jnp.zeros_like(run_ref)

    h1 = x_ref[...] + jnp.dot(m_ref[...], wo_ref[...], preferred_element_type=F32)
    h1_ref[...] = h1
    ms = jnp.mean(h1 * h1, axis=-1, keepdims=True)
    hn = h1 * lax.rsqrt(ms + RMS_EPS) * g2_ref[...]
    hn_ref[...] = hn
    wr = wr_ref[...]
    hn_hi = hn.astype(BF16)
    hn_lo = (hn - hn_hi.astype(F32)).astype(BF16)
    wr_hi = wr.astype(BF16)
    wr_lo = (wr - wr_hi.astype(F32)).astype(BF16)
    logits = (jnp.dot(hn_hi, wr_hi, preferred_element_type=F32)
              + jnp.dot(hn_hi, wr_lo, preferred_element_type=F32)
              + jnp.dot(hn_lo, wr_hi, preferred_element_type=F32)) + br_ref[...]

    R = ROUTE_ROWS
    lanes = lax.broadcasted_iota(I32, (R, N_EXPERTS), 1)
    work = logits
    vals, sels, idxs = [], [], []
    for _ in range(TOP_K):
        m = jnp.max(work, axis=-1, keepdims=True)
        idx = jnp.min(jnp.where(work == m, lanes, N_EXPERTS), axis=-1, keepdims=True)
        sel = lanes == idx
        vals.append(m)
        idxs.append(idx)
        sels.append(sel)
        work = jnp.where(sel, -jnp.inf, work)
    es = [jnp.exp(v - vals[0]) for v in vals]
    denom = es[0] + es[1] + es[2] + es[3]

    hot = jnp.zeros((R, N_EXPERTS), F32)
    for sel in sels:
        hot = jnp.where(sel, 1.0, hot)
    ri = lax.broadcasted_iota(I32, (R, R), 0)
    ci = lax.broadcasted_iota(I32, (R, R), 1)
    strict = jnp.where(ci < ri, 1.0, 0.0).astype(BF16)
    before = jnp.dot(strict, hot.astype(BF16), preferred_element_type=F32) + run_ref[...]
    run_new = run_ref[...] + jnp.sum(hot, axis=0, keepdims=True)
    run_ref[...] = run_new
    cnt_ref[...] = run_new

    ol = lax.broadcasted_iota(I32, (R, ROUTE_LANES), 1)
    idx_o = jnp.zeros((R, ROUTE_LANES), I32)
    wt_o = jnp.zeros((R, ROUTE_LANES), F32)
    rank_o = jnp.zeros((R, ROUTE_LANES), I32)
    for k in range(TOP_K):
        rk = jnp.sum(jnp.where(sels[k], before, 0.0), axis=-1, keepdims=True).astype(I32)
        idx_o = jnp.where(ol == k, idxs[k], idx_o)
        wt_o = jnp.where(ol == k, es[k] / denom, wt_o)
        rank_o = jnp.where(ol == k, rk, rank_o)
    idx_ref[...] = idx_o
    wt_ref[...] = wt_o
    rank_ref[...] = rank_o


def _route(merged, x2, wo, g2, wr, br):
    T = x2.shape[0]
    R = ROUTE_ROWS
    row = lambda i: (i, 0)
    fixed = lambda i: (0, 0)
    return pl.pallas_call(
        _route_kernel,
        grid=(T // R,),
        in_specs=[pl.BlockSpec((R, D_MODEL), row),
                  pl.BlockSpec((R, D_MODEL), row),
                  pl.BlockSpec((D_MODEL, D_MODEL), fixed),
                  pl.BlockSpec((1, D_MODEL), fixed),
                  pl.BlockSpec((D_MODEL, N_EXPERTS), fixed),
                  pl.BlockSpec((1, N_EXPERTS), fixed)],
        out_specs=[pl.BlockSpec((R, D_MODEL), row),
                   pl.BlockSpec((R, D_MODEL), row),
                   pl.BlockSpec((R, ROUTE_LANES), row),
                   pl.BlockSpec((R, ROUTE_LANES), row),
                   pl.BlockSpec((R, ROUTE_LANES), row),
                   pl.BlockSpec((1, N_EXPERTS), fixed)],
        out_shape=[jax.ShapeDtypeStruct((T, D_MODEL), F32),
                   jax.ShapeDtypeStruct((T, D_MODEL), F32),
                   jax.ShapeDtypeStruct((T, ROUTE_LANES), I32),
                   jax.ShapeDtypeStruct((T, ROUTE_LANES), F32),
                   jax.ShapeDtypeStruct((T, ROUTE_LANES), I32),
                   jax.ShapeDtypeStruct((1, N_EXPERTS), F32)],
        scratch_shapes=[pltpu.VMEM((1, N_EXPERTS), F32)],
        compiler_params=_cparams(("arbitrary",)),
        name="outproj_route",
    )(merged, x2, wo, g2, wr, br)


DISP_ROWS = 512
DMA_UNROLL = 8


def _dispatch_kernel(pos_ref, hn_ref, xs_ref, sem):
    def row_copy(g, tt, k):
        dst = pos_ref[0, 0, (g * DMA_UNROLL + tt) * TOP_K + k]
        return pltpu.make_async_copy(hn_ref.at[g, pl.ds(tt, 1), :], xs_ref.at[pl.ds(dst, 1), :], sem)

    def issue(g, carry):
        for tt in range(DMA_UNROLL):
            for k in range(TOP_K):
                row_copy(g, tt, k).start(priority=k % 2)
        return carry

    lax.fori_loop(0, DISP_ROWS // DMA_UNROLL, issue, 0)

    def drain(g, carry):
        for tt in range(DMA_UNROLL):
            for k in range(TOP_K):
                row_copy(g, tt, k).wait()
        return carry

    lax.fori_loop(0, DISP_ROWS // DMA_UNROLL, drain, 0)


def _dispatch(hn, pos, n_alloc):
    T = hn.shape[0]
    R = DISP_ROWS
    G = DMA_UNROLL
    pos3 = pos.reshape(T // R, 1, R * TOP_K)
    return pl.pallas_call(
        _dispatch_kernel,
        grid=(T // R,),
        in_specs=[pl.BlockSpec((1, 1, R * TOP_K), lambda i: (i, 0, 0), memory_space=pltpu.SMEM),
                  pl.BlockSpec((R // G, G, D_MODEL), lambda i: (i, 0, 0))],
        out_specs=pl.BlockSpec(memory_space=pl.ANY),
        out_shape=jax.ShapeDtypeStruct((n_alloc, D_MODEL), F32),
        scratch_shapes=[pltpu.SemaphoreType.DMA(())],
        compiler_params=_cparams(("arbitrary",)),
        name="dispatch",
    )(pos3, hn.reshape(T // G, G, D_MODEL))


def _expert_kernel(ue_ref, ur_ref, un_ref, xs_ref, wg_ref, wu_ref, wd_ref, bg_ref, bu_ref, bd_ref,
                   ys_ref, xb_ref, h_ref, stage_ref, ybuf_ref, pend_ref, xsem, ysem):
    u = pl.program_id(0)
    s = pl.program_id(1)
    nu = pl.num_programs(0)
    nblk = un_ref[u]
    row0 = ur_ref[u]
    slot = u % 2

    def blk_rows(b):
        return pl.ds(pl.multiple_of(b * ROW_BLK, ROW_BLK), ROW_BLK)

    def for_row_blocks(fn):
        def pair(i, carry):
            fn(pl.ds(pl.multiple_of(i * (2 * ROW_BLK), 2 * ROW_BLK), 2 * ROW_BLK))
            return carry

        lax.fori_loop(0, lax.shift_right_logical(nblk, 1), pair, 0)

        @pl.when((nblk & 1) == 1)
        def _():
            fn(blk_rows(nblk - 1))

    def x_copy(unit, b):
        rows = pl.ds(pl.multiple_of(ur_ref[unit] + b * ROW_BLK, ROW_BLK), ROW_BLK)
        return pltpu.make_async_copy(xs_ref.at[rows, :], stage_ref, xsem)

    def y_copy(b, n, ys):
        rows = pl.ds(pl.multiple_of(row0 + b * ROW_BLK, ROW_BLK), ROW_BLK)
        cols = pl.ds(pl.multiple_of(n * FF_TILE, FF_TILE), FF_TILE)
        return pltpu.make_async_copy(ybuf_ref.at[ys, blk_rows(b), :], ys_ref.at[rows, cols], ysem)

    def drain_stores():
        def wait_one(i, carry):
            y_copy(0, 0, 0).wait()
            return carry

        lax.fori_loop(0, pend_ref[0], wait_one, 0)
        pend_ref[0] = 0

    @pl.when((u == 0) & (s == 0))
    def _():
        pend_ref[0] = 0

        def load(b, carry):
            cp = x_copy(0, b)
            cp.start()
            cp.wait()
            xb_ref[0, blk_rows(b), :] = stage_ref[...].astype(BF16)
            return carry

        lax.fori_loop(0, nblk, load, 0)

    nxt = jnp.minimum(u + 1, nu - 1)
    prefetch = (u + 1 < nu) & (s < un_ref[nxt])

    @pl.when(prefetch)
    def _():
        x_copy(nxt, s).start()

    @pl.when(s < NJ)
    def _():
        wg = wg_ref[...].astype(BF16)
        wu = wu_ref[...].astype(BF16)
        bg = bg_ref[...]
        bu = bu_ref[...]

        def up_rows(rows):
            xb = xb_ref[slot, rows, :]
            g = jnp.minimum(jnp.dot(xb, wg, preferred_element_type=F32) + bg, SWIGLU_LIMIT)
            up = jnp.clip(jnp.dot(xb, wu, preferred_element_type=F32) + bu,
                          -SWIGLU_LIMIT, SWIGLU_LIMIT)
            h_ref[s, rows, :] = ((up + 1.0) * (g * jax.nn.sigmoid(SWIGLU_ALPHA * g))).astype(BF16)

        for_row_blocks(up_rows)

    @pl.when(s >= NJ)
    def _():
        n = s - NJ
        ys = n % 2
        wd = wd_ref[...].astype(BF16)
        bd = bd_ref[...]

        def down_rows(rows):
            hb = jnp.concatenate([h_ref[t, rows, :] for t in range(NJ)], axis=1)
            ybuf_ref[ys, rows, :] = jnp.dot(hb, wd, preferred_element_type=F32) + bd

        for_row_blocks(down_rows)
        drain_stores()

        def issue(b, carry):
            y_copy(b, n, ys).start()
            return carry

        lax.fori_loop(0, nblk, issue, 0)
        pend_ref[0] = nblk

    @pl.when(prefetch)
    def _():
        x_copy(nxt, s).wait()
        xb_ref[1 - slot, blk_rows(s), :] = stage_ref[...].astype(BF16)

    @pl.when((u == nu - 1) & (s == 2 * NJ - 1))
    def _():
        drain_stores()


def _experts(xs, unit_e, unit_row0, unit_nblk, w_gate, w_up, w_down, b_gate, b_up, b_down):
    n_alloc = xs.shape[0]
    n_units = unit_e.shape[0]
    last = NJ - 1

    def up_tile(u, s, ue, ur, un):
        return (ue[u], 0, jnp.where(un[u] > 0, jnp.minimum(s, last), last))

    def down_tile(u, s, ue, ur, un):
        return (ue[u], 0, jnp.where(un[u] > 0, jnp.maximum(s - NJ, 0), last))

    grid_spec = pltpu.PrefetchScalarGridSpec(
        num_scalar_prefetch=3,
        grid=(n_units, 2 * NJ),
        in_specs=[
            pl.BlockSpec(memory_space=pl.ANY),
            pl.BlockSpec((None, D_MODEL, FF_TILE), up_tile),
            pl.BlockSpec((None, D_MODEL, FF_TILE), up_tile),
            pl.BlockSpec((None, D_FF, FF_TILE), down_tile),
            pl.BlockSpec((None, 1, FF_TILE), up_tile),
            pl.BlockSpec((None, 1, FF_TILE), up_tile),
            pl.BlockSpec((None, 1, FF_TILE), down_tile),
        ],
        out_specs=pl.BlockSpec(memory_space=pl.ANY),
        scratch_shapes=[pltpu.VMEM((2, SEG_ROWS, D_MODEL), BF16),
                        pltpu.VMEM((NJ, SEG_ROWS, FF_TILE), BF16),
                        pltpu.VMEM((ROW_BLK, D_MODEL), F32),
                        pltpu.VMEM((2, SEG_ROWS, FF_TILE), F32),
                        pltpu.SMEM((1,), I32),
                        pltpu.SemaphoreType.DMA(()),
                        pltpu.SemaphoreType.DMA(())],
    )
    return pl.pallas_call(
        _expert_kernel,
        grid_spec=grid_spec,
        out_shape=jax.ShapeDtypeStruct((n_alloc, D_MODEL), F32),
        compiler_params=_cparams(("arbitrary", "arbitrary")),
        name="experts",
    )(unit_e, unit_row0, unit_nblk, xs, w_gate, w_up, w_down,
      b_gate.reshape(N_EXPERTS, 1, D_FF), b_up.reshape(N_EXPERTS, 1, D_FF),
      b_down.reshape(N_EXPERTS, 1, D_MODEL))


COMB_ROWS = 128


def _combine_kernel(pos_ref, nxt_ref, h1_ref, wt_ref, ys_ref, o_ref, buf_ref, sems):
    i = pl.program_id(0)
    n = pl.num_programs(0)
    slot = i % 2

    def row_copy(p_ref, s, g, tt, k):
        src = p_ref[0, 0, (g * DMA_UNROLL + tt) * TOP_K + k]
        return pltpu.make_async_copy(ys_ref.at[pl.ds(src, 1), :],
                                     buf_ref.at[s, k, g, pl.ds(tt, 1), :], sems.at[s])

    def issue_all(p_ref, s):
        def issue(g, carry):
            for tt in range(DMA_UNROLL):
                for k in range(TOP_K):
                    row_copy(p_ref, s, g, tt, k).start(priority=k % 2)
            return carry

        lax.fori_loop(0, COMB_ROWS // DMA_UNROLL, issue, 0)

    @pl.when(i == 0)
    def _():
        issue_all(pos_ref, 0)

    @pl.when(i + 1 < n)
    def _():
        issue_all(nxt_ref, 1 - slot)

    def drain(g, carry):
        for tt in range(DMA_UNROLL):
            for k in range(TOP_K):
                row_copy(pos_ref, slot, g, tt, k).wait()
        return carry

    lax.fori_loop(0, COMB_ROWS // DMA_UNROLL, drain, 0)

    wt = wt_ref[...]
    acc = h1_ref[...]
    for k in range(TOP_K):
        acc = acc + wt[:, :, k:k + 1] * buf_ref[slot, k]
    o_ref[...] = acc


def _combine(h1, wt, ys, pos):
    T = h1.shape[0]
    R = COMB_ROWS
    G = DMA_UNROLL
    nsteps = T // R
    pos3 = pos.reshape(nsteps, 1, R * TOP_K)
    tile = lambda w: pl.BlockSpec((R // G, G, w), lambda i: (i, 0, 0))
    out = pl.pallas_call(
        _combine_kernel,
        grid=(nsteps,),
        in_specs=[pl.BlockSpec((1, 1, R * TOP_K), lambda i: (i, 0, 0), memory_space=pltpu.SMEM),
                  pl.BlockSpec((1, 1, R * TOP_K), lambda i: (jnp.minimum(i + 1, nsteps - 1), 0, 0),
                               memory_space=pltpu.SMEM),
                  tile(D_MODEL),
                  tile(ROUTE_LANES),
                  pl.BlockSpec(memory_space=pl.ANY)],
        out_specs=tile(D_MODEL),
        out_shape=jax.ShapeDtypeStruct((T // G, G, D_MODEL), F32),
        scratch_shapes=[pltpu.VMEM((2, TOP_K, R // G, G, D_MODEL), F32),
                        pltpu.SemaphoreType.DMA((2,))],
        compiler_params=_cparams(("arbitrary",)),
        name="combine",
    )(pos3, pos3, h1.reshape(T // G, G, D_MODEL), wt.reshape(T // G, G, ROUTE_LANES), ys)
    return out.reshape(T, D_MODEL)


def _max_units(T):
    max_blocks = (T * TOP_K) // ROW_BLK + N_EXPERTS
    return (max_blocks + N_EXPERTS * (SEG_BLKS - 1)) // SEG_BLKS


def _routing_tables(counts, idx, rank, T):
    counts = counts.astype(I32)
    nblk = (counts + ROW_BLK - 1) // ROW_BLK
    pstart = (jnp.cumsum(nblk) - nblk) * ROW_BLK
    hot = idx[..., None] == jnp.arange(N_EXPERTS, dtype=I32)
    pos = jnp.sum(jnp.where(hot, pstart, 0), axis=-1) + rank
    units = (nblk + SEG_BLKS - 1) // SEG_BLKS
    uend = jnp.cumsum(units)
    ustart = uend - units
    n_units = _max_units(T)
    u = jnp.arange(n_units, dtype=I32)
    total = uend[-1]
    ue = jnp.minimum(jnp.searchsorted(uend, u, side="right").astype(I32), N_EXPERTS - 1)
    local = u - jnp.take(ustart, ue)
    un = jnp.clip(jnp.take(nblk, ue) - local * SEG_BLKS, 0, SEG_BLKS)
    ur = jnp.take(pstart, ue) + local * SEG_ROWS
    live = u < total
    last_e = jnp.take(ue, jnp.maximum(total - 1, 0))
    ue = jnp.where(live, ue, last_e)
    un = jnp.where(live, un, 0)
    ur = jnp.where(live, ur, 0)
    return pos.astype(I32), ue, ur.astype(I32), un.astype(I32)


def kernel(x, norm1_g, w_in, w_gla_gate_up, b_gla_gate, gla_norm_g, q_norm_g, k_norm_g, rel_bias,
           b_branch_gate, w_branch_gla, w_branch_att, w_out, norm2_g, w_router, b_router,
           w_gate, b_gate, w_up, b_up, w_down, b_down):
    B, S, D = x.shape
    T = B * S
    h = x.reshape(T, D)
    wt_in = jnp.swapaxes(w_in, 1, 2)
    for l in range(w_in.shape[0]):
        xn, glr = _norm1(h, norm1_g[l].reshape(1, D), wt_in[l, COL_LR:COL_B, :].astype(BF16))
        proj_a = _proj(xn, wt_in[l], 0, COLS_A, "inproj_gla")
        proj_b = _proj(xn, wt_in[l], COL_B, COLS_B, "inproj_att")
        y_gla = _gla(proj_a, glr, w_gla_gate_up[l], b_gla_gate[l].reshape(1, GLA_QK),
                     gla_norm_g[l].reshape(1, GLA_DV))
        y_att = _attention(proj_b, q_norm_g[l].reshape(1, ATT_HD), k_norm_g[l].reshape(1, ATT_HD),
                           _att_bias_diagonals(rel_bias[l]))
        merged = _merge(y_gla, y_att, w_branch_gla[l].astype(BF16), w_branch_att[l].astype(BF16),
                        proj_b, b_branch_gate[l])
        h1, hn, idx, wt, rank, counts = _route(
            merged, h, w_out[l].astype(BF16), norm2_g[l].reshape(1, D), w_router[l],
            b_router[l].reshape(1, N_EXPERTS))
        pos, ue, ur, un = _routing_tables(counts[0], idx[:, :TOP_K], rank[:, :TOP_K], T)
        n_alloc = ((T * TOP_K) // ROW_BLK + N_EXPERTS) * ROW_BLK
        xs = _dispatch(hn, pos.reshape(-1), n_alloc)
        ys = _experts(xs, ue, ur, un, w_gate[l], w_up[l], w_down[l], b_gate[l], b_up[l], b_down[l])
        h = _combine(h1, wt, ys, pos.reshape(-1))
    return h.reshape(B, S, D)
```

```python
import functools

import jax
import jax.numpy as jnp
import numpy as np
from jax import lax
from jax.experimental import pallas as pl
from jax.experimental.pallas import tpu as pltpu

F32 = jnp.float32
BF16 = jnp.bfloat16
I32 = jnp.int32

D_MODEL = 2048
CHUNK = 64
RMS_EPS = 1e-6
GLA_HEADS = 4
GLA_DK = 256
GLA_DV = 512
GLA_RANK = 16
GLA_TAU = 16.0
GLA_QK = GLA_HEADS * GLA_DK
GLA_V = GLA_HEADS * GLA_DV
ATT_HEADS = 8
ATT_HD = 128
ATT_W = ATT_HEADS * ATT_HD
ATT_LEFT = 8
REL_CLIP = 128
N_EXPERTS = 32
TOP_K = 4
D_FF = 2048
SWIGLU_LIMIT = 7.0
SWIGLU_ALPHA = 1.702

COLS_A = 2 * GLA_QK + 2 * GLA_V
COL_LR = COLS_A
COL_B = COLS_A + GLA_RANK
COLS_B = 3 * ATT_W + 2 * D_MODEL

VMEM_LIMIT = 56 * 1024 * 1024
NEG_BIG = -1e30

ROW_BLK = 256
SEG_BLKS = 5
SEG_ROWS = ROW_BLK * SEG_BLKS
FF_TILE = 512
NJ = D_FF // FF_TILE
assert D_MODEL // FF_TILE == NJ and SEG_BLKS <= 2 * NJ


def _cparams(sem):
    return pltpu.CompilerParams(dimension_semantics=sem, vmem_limit_bytes=VMEM_LIMIT)


def _norm1_kernel(x_ref, g_ref, wlr_ref, xn_ref, glr_ref):
    x = x_ref[...]
    ms = jnp.mean(x * x, axis=-1, keepdims=True)
    y = (x * lax.rsqrt(ms + RMS_EPS) * g_ref[...]).astype(BF16)
    xn_ref[...] = y
    glr_ref[...] = lax.dot_general(y, wlr_ref[...], (((1,), (1,)), ((), ())),
                                   preferred_element_type=F32)


def _norm1(x2, g, wlr):
    T = x2.shape[0]
    tm = 512
    return pl.pallas_call(
        _norm1_kernel,
        grid=(T // tm,),
        in_specs=[pl.BlockSpec((tm, D_MODEL), lambda i: (i, 0)),
                  pl.BlockSpec((1, D_MODEL), lambda i: (0, 0)),
                  pl.BlockSpec((GLA_RANK, D_MODEL), lambda i: (0, 0))],
        out_specs=[pl.BlockSpec((tm, D_MODEL), lambda i: (i, 0)),
                   pl.BlockSpec((tm, GLA_RANK), lambda i: (i, 0))],
        out_shape=[jax.ShapeDtypeStruct((T, D_MODEL), BF16),
                   jax.ShapeDtypeStruct((T, GLA_RANK), F32)],
        compiler_params=_cparams(("parallel",)),
        name="norm1",
    )(x2, g, wlr)


def _proj_kernel(x_ref, w_ref, o_ref, wb_ref):
    @pl.when(pl.program_id(1) == 0)
    def _():
        wb_ref[...] = w_ref[...].astype(BF16)

    o_ref[...] = lax.dot_general(x_ref[...], wb_ref[...], (((1,), (1,)), ((), ())),
                                 preferred_element_type=F32).astype(o_ref.dtype)


def _proj(xn, wt, col0, n_cols, name):
    T, K = xn.shape
    tm, tn = 1024, 1024
    assert col0 % 8 == 0
    return pl.pallas_call(
        _proj_kernel,
        grid=(n_cols // tn, T // tm),
        in_specs=[pl.BlockSpec((tm, K), lambda j, i: (i, 0)),
                  pl.BlockSpec((pl.Element(tn), pl.Element(K)),
                               lambda j, i: (pl.multiple_of(col0 + j * tn, 8), 0))],
        out_specs=pl.BlockSpec((tm, tn), lambda j, i: (i, j)),
        out_shape=jax.ShapeDtypeStruct((T, n_cols), BF16),
        scratch_shapes=[pltpu.VMEM((tn, K), BF16)],
        compiler_params=_cparams(("arbitrary", "arbitrary")),
        name=name,
    )(xn, wt)


GLA_ROWS = 256


def _gla_kernel(q_ref, k_ref, v_ref, r_ref, glr_ref, wup_ref, bg_ref, ng_ref, o_ref, st_ref):
    @pl.when(pl.program_id(0) == 0)
    def _():
        st_ref[...] = jnp.zeros_like(st_ref)

    z = jnp.dot(glr_ref[...], wup_ref[...], preferred_element_type=F32,
                precision=lax.Precision.HIGHEST) + bg_ref[...]
    log_a = (jnp.minimum(z, 0.0) - jnp.log(1.0 + jnp.exp(-jnp.abs(z)))) * (1.0 / GLA_TAU)

    ri = lax.broadcasted_iota(I32, (CHUNK, CHUNK), 0)
    ci = lax.broadcasted_iota(I32, (CHUNK, CHUNK), 1)
    causal = ci <= ri
    tril = jnp.where(causal, 1.0, 0.0).astype(BF16)
    scale = GLA_DK ** -0.5
    ng = ng_ref[...]

    for c in range(GLA_ROWS // CHUNK):
        rows = slice(c * CHUNK, (c + 1) * CHUNK)
        la = log_a[rows]
        la_hi = la.astype(BF16)
        la_lo = (la - la_hi.astype(F32)).astype(BF16)
        bcum_all = (jnp.dot(tril, la_hi, preferred_element_type=F32)
                    + jnp.dot(tril, la_lo, preferred_element_type=F32))
        for h in range(GLA_HEADS):
            kcols = slice(h * GLA_DK, (h + 1) * GLA_DK)
            vcols = slice(h * GLA_DV, (h + 1) * GLA_DV)
            bcum = bcum_all[:, kcols]
            blast = bcum[CHUNK - 1:CHUNK, :]
            qc = q_ref[rows, kcols].astype(F32) * scale
            kc = k_ref[rows, kcols].astype(F32)
            q_dec = (qc * jnp.exp(bcum)).astype(BF16)
            k_inv = (kc * jnp.exp(-bcum)).astype(BF16)
            k_end = (kc * jnp.exp(blast - bcum)).astype(BF16)
            vv = v_ref[rows, vcols]
            a = lax.dot_general(q_dec, k_inv, (((1,), (1,)), ((), ())), preferred_element_type=F32)
            a = jnp.where(causal, a, 0.0).astype(BF16)
            st = st_ref[h]
            o = jnp.dot(a, vv, preferred_element_type=F32)
            o = o + lax.dot_general(q_dec, st.astype(BF16), (((1,), (1,)), ((), ())),
                                    preferred_element_type=F32)
            st_ref[h] = st * jnp.exp(blast) + lax.dot_general(
                vv, k_end, (((0,), (0,)), ((), ())), preferred_element_type=F32)
            ms = jnp.mean(o * o, axis=-1, keepdims=True)
            y = o * lax.rsqrt(ms + RMS_EPS) * ng
            rr = r_ref[rows, vcols].astype(F32)
            o_ref[rows, vcols] = (y * (rr * jax.nn.sigmoid(rr))).astype(o_ref.dtype)


def _gla(proj_a, glr, wup, bg, ng):
    T = proj_a.shape[0]
    R = GLA_ROWS
    return pl.pallas_call(
        _gla_kernel,
        grid=(T // R,),
        in_specs=[pl.BlockSpec((R, GLA_QK), lambda n: (n, 0)),
                  pl.BlockSpec((R, GLA_QK), lambda n: (n, 1)),
                  pl.BlockSpec((R, GLA_V), lambda n: (n, 1)),
                  pl.BlockSpec((R, GLA_V), lambda n: (n, 2)),
                  pl.BlockSpec((R, GLA_RANK), lambda n: (n, 0)),
                  pl.BlockSpec((GLA_RANK, GLA_QK), lambda n: (0, 0)),
                  pl.BlockSpec((1, GLA_QK), lambda n: (0, 0)),
                  pl.BlockSpec((1, GLA_DV), lambda n: (0, 0))],
        out_specs=pl.BlockSpec((R, GLA_V), lambda n: (n, 0)),
        out_shape=jax.ShapeDtypeStruct((T, GLA_V), BF16),
        scratch_shapes=[pltpu.VMEM((GLA_HEADS, GLA_DV, GLA_DK), F32)],
        compiler_params=_cparams(("arbitrary",)),
        name="gla",
    )(proj_a, proj_a, proj_a, proj_a, glr, wup, bg, ng)


ATT_QROWS = 256
ATT_PAD = ATT_LEFT * CHUNK
ATT_BAND = ATT_PAD + ATT_QROWS


ATT_EXT = 1024
ATT_GROUP = 2


def _att_bias_diagonals(rel_bias):
    m = np.arange(ATT_EXT)
    m = np.where(m < ATT_BAND, m, m - ATT_EXT)
    rel = np.clip(ATT_PAD - m, -REL_CLIP, REL_CLIP) + REL_CLIP
    return rel_bias.astype(F32)[:, rel].reshape(ATT_HEADS, 1, ATT_EXT)


def _att_kernel(q_ref, k_ref, v_ref, qg_ref, kg_ref, ext_ref, o_ref, kn_ref, vp_ref, tab_ref):
    qb = pl.program_id(1)

    @pl.when(qb == 0)
    def _():
        qc = lax.broadcasted_iota(I32, (ATT_QROWS, ATT_BAND), 0) // CHUNK
        kc = lax.broadcasted_iota(I32, (ATT_QROWS, ATT_BAND), 1) // CHUNK
        in_band = (kc >= qc) & (kc <= qc + ATT_LEFT)
        for g in range(ATT_GROUP):
            cols = slice(g * ATT_HD, (g + 1) * ATT_HD)
            ext = jnp.broadcast_to(ext_ref[g], (ATT_QROWS, ATT_EXT))
            tab = pltpu.roll(ext, 0, 1, stride=1, stride_axis=0)[:, :ATT_BAND]
            tab_ref[g] = jnp.where(in_band, tab, NEG_BIG)
            kk = k_ref[:, cols].astype(F32)
            ms = jnp.mean(kk * kk, axis=-1, keepdims=True)
            kn = kk * lax.rsqrt(ms + RMS_EPS) * kg_ref[...]
            kn_ref[g, 0:ATT_PAD, :] = jnp.zeros((ATT_PAD, ATT_HD), BF16)
            kn_ref[g, ATT_PAD:, :] = kn.astype(BF16)
            vp_ref[g, 0:ATT_PAD, :] = jnp.zeros((ATT_PAD, ATT_HD), BF16)
            vp_ref[g, ATT_PAD:, :] = v_ref[:, cols]

    start = pl.multiple_of(qb * ATT_QROWS, ATT_QROWS)
    kpos = start - ATT_PAD + lax.broadcasted_iota(I32, (1, ATT_BAND), 1)
    for g in range(ATT_GROUP):
        cols = slice(g * ATT_HD, (g + 1) * ATT_HD)
        q = q_ref[:, cols].astype(F32)
        ms = jnp.mean(q * q, axis=-1, keepdims=True)
        qn = (q * lax.rsqrt(ms + RMS_EPS) * qg_ref[...] * (ATT_HD ** -0.5)).astype(BF16)
        kb = kn_ref[g, pl.ds(start, ATT_BAND), :]
        vb = vp_ref[g, pl.ds(start, ATT_BAND), :]
        s = lax.dot_general(qn, kb, (((1,), (1,)), ((), ())), preferred_element_type=F32)
        s = s + tab_ref[g]
        s = jnp.where(kpos >= 0, s, NEG_BIG)
        m = jnp.max(s, axis=-1, keepdims=True)
        p = jnp.exp(s - m)
        l = jnp.sum(p, axis=-1, keepdims=True)
        o = jnp.dot(p.astype(BF16), vb, preferred_element_type=F32)
        o_ref[:, cols] = (o / l).astype(o_ref.dtype)


def _attention(proj_b, qg, kg, ext):
    T = proj_b.shape[0]
    R = ATT_QROWS
    GW = ATT_GROUP * ATT_HD
    NG = ATT_HEADS // ATT_GROUP
    return pl.pallas_call(
        _att_kernel,
        grid=(NG, T // R),
        in_specs=[pl.BlockSpec((R, GW), lambda h, n: (n, h)),
                  pl.BlockSpec((T, GW), lambda h, n: (0, NG + h)),
                  pl.BlockSpec((T, GW), lambda h, n: (0, 2 * NG + h)),
                  pl.BlockSpec((1, ATT_HD), lambda h, n: (0, 0)),
                  pl.BlockSpec((1, ATT_HD), lambda h, n: (0, 0)),
                  pl.BlockSpec((ATT_GROUP, 1, ATT_EXT), lambda h, n: (h, 0, 0))],
        out_specs=pl.BlockSpec((R, GW), lambda h, n: (n, h)),
        out_shape=jax.ShapeDtypeStruct((T, ATT_W), BF16),
        scratch_shapes=[pltpu.VMEM((ATT_GROUP, T + ATT_PAD, ATT_HD), BF16),
                        pltpu.VMEM((ATT_GROUP, T + ATT_PAD, ATT_HD), BF16),
                        pltpu.VMEM((ATT_GROUP, R, ATT_BAND), F32)],
        compiler_params=_cparams(("arbitrary", "arbitrary")),
        name="chunk_attention",
    )(proj_b, proj_b, proj_b, qg, kg, ext)


def _merge_kernel(yg_ref, ya_ref, wg_ref, wa_ref, gg_ref, ga_ref, bb_ref, o_ref):
    pg = jnp.dot(yg_ref[...], wg_ref[...], preferred_element_type=F32)
    pa = jnp.dot(ya_ref[...], wa_ref[...], preferred_element_type=F32)
    bb = bb_ref[...]
    sg = jax.nn.sigmoid(gg_ref[...].astype(F32) + bb[0:1, :])
    sa = jax.nn.sigmoid(ga_ref[...].astype(F32) + bb[1:2, :])
    o_ref[...] = (sg * pg + sa * pa).astype(o_ref.dtype)


def _merge(y_gla, y_att, wg, wa, proj_b, bb):
    T = y_gla.shape[0]
    tm, tn = 1024, 1024
    gate0 = 3 * ATT_W // tn
    return pl.pallas_call(
        _merge_kernel,
        grid=(D_MODEL // tn, T // tm),
        in_specs=[pl.BlockSpec((tm, GLA_V), lambda j, i: (i, 0)),
                  pl.BlockSpec((tm, ATT_W), lambda j, i: (i, 0)),
                  pl.BlockSpec((GLA_V, tn), lambda j, i: (0, j)),
                  pl.BlockSpec((ATT_W, tn), lambda j, i: (0, j)),
                  pl.BlockSpec((tm, tn), lambda j, i: (i, gate0 + j)),
                  pl.BlockSpec((tm, tn), lambda j, i: (i, gate0 + D_MODEL // tn + j)),
                  pl.BlockSpec((2, tn), lambda j, i: (0, j))],
        out_specs=pl.BlockSpec((tm, tn), lambda j, i: (i, j)),
        out_shape=jax.ShapeDtypeStruct((T, D_MODEL), BF16),
        compiler_params=_cparams(("arbitrary", "arbitrary")),
        name="merge",
    )(y_gla, y_att, wg, wa, proj_b, proj_b, bb)


ROUTE_ROWS = 512
ROUTE_SUB = 256
ROUTE_LANES = 128


def _route_kernel(m_ref, x_ref, wo_ref, g2_ref, wr_ref, br_ref,
                  h1_ref, hn_ref, idx_ref, wt_ref, rank_ref, cnt_ref, run_ref):
    @pl.when(pl.program_id(0) == 0)
    def _():
        run_ref[...] = jnp.zeros_like(run_ref)

    run = run_ref[...]
    for sub in range(ROUTE_ROWS // ROUTE_SUB):
        rows = slice(sub * ROUTE_SUB, (sub + 1) * ROUTE_SUB)
        h1 = x_ref[rows, :] + jnp.dot(m_ref[rows, :], wo_ref[...], preferred_element_type=F32)
        h1_ref[rows, :] = h1
        ms = jnp.mean(h1 * h1, axis=-1, keepdims=True)
        hn = h1 * lax.rsqrt(ms + RMS_EPS) * g2_ref[...]
        hn_ref[rows, :] = hn
        wr = wr_ref[...]
        hn_hi = hn.astype(BF16)
        hn_lo = (hn - hn_hi.astype(F32)).astype(BF16)
        wr_hi = wr.astype(BF16)
        wr_lo = (wr - wr_hi.astype(F32)).astype(BF16)
        logits = (jnp.dot(hn_hi, wr_hi, preferred_element_type=F32)
                  + jnp.dot(hn_hi, wr_lo, preferred_element_type=F32)
                  + jnp.dot(hn_lo, wr_hi, preferred_element_type=F32)) + br_ref[...]

        R = ROUTE_SUB
        lanes = lax.broadcasted_iota(I32, (R, N_EXPERTS), 1)
        work = logits
        vals, sels, idxs = [], [], []
        for _ in range(TOP_K):
            m = jnp.max(work, axis=-1, keepdims=True)
            idx = jnp.min(jnp.where(work == m, lanes, N_EXPERTS), axis=-1, keepdims=True)
            sel = lanes == idx
            vals.append(m)
            idxs.append(idx)
            sels.append(sel)
            work = jnp.where(sel, -jnp.inf, work)
        es = [jnp.exp(v - vals[0]) for v in vals]
        denom = es[0] + es[1] + es[2] + es[3]

        hot = jnp.zeros((R, N_EXPERTS), F32)
        for sel in sels:
            hot = jnp.where(sel, 1.0, hot)
        ri = lax.broadcasted_iota(I32, (R, R), 0)
        ci = lax.broadcasted_iota(I32, (R, R), 1)
        strict = jnp.where(ci < ri, 1.0, 0.0).astype(BF16)
        before = jnp.dot(strict, hot.astype(BF16), preferred_element_type=F32) + run
        run = run + jnp.sum(hot, axis=0, keepdims=True)

        ol = lax.broadcasted_iota(I32, (R, ROUTE_LANES), 1)
        idx_o = jnp.zeros((R, ROUTE_LANES), I32)
        wt_o = jnp.zeros((R, ROUTE_LANES), F32)
        rank_o = jnp.zeros((R, ROUTE_LANES), I32)
        for k in range(TOP_K):
            rk = jnp.sum(jnp.where(sels[k], before, 0.0), axis=-1, keepdims=True).astype(I32)
            idx_o = jnp.where(ol == k, idxs[k], idx_o)
            wt_o = jnp.where(ol == k, es[k] / denom, wt_o)
            rank_o = jnp.where(ol == k, rk, rank_o)
        idx_ref[rows, :] = idx_o
        wt_ref[rows, :] = wt_o
        rank_ref[rows, :] = rank_o
    run_ref[...] = run
    cnt_ref[...] = run


def _route(merged, x2, wo, g2, wr, br):
    T = x2.shape[0]
    R = ROUTE_ROWS
    row = lambda i: (i, 0)
    fixed = lambda i: (0, 0)
    return pl.pallas_call(
        _route_kernel,
        grid=(T // R,),
        in_specs=[pl.BlockSpec((R, D_MODEL), row),
                  pl.BlockSpec((R, D_MODEL), row),
                  pl.BlockSpec((D_MODEL, D_MODEL), fixed),
                  pl.BlockSpec((1, D_MODEL), fixed),
                  pl.BlockSpec((D_MODEL, N_EXPERTS), fixed),
                  pl.BlockSpec((1, N_EXPERTS), fixed)],
        out_specs=[pl.BlockSpec((R, D_MODEL), row),
                   pl.BlockSpec((R, D_MODEL), row),
                   pl.BlockSpec((R, ROUTE_LANES), row),
                   pl.BlockSpec((R, ROUTE_LANES), row),
                   pl.BlockSpec((R, ROUTE_LANES), row),
                   pl.BlockSpec((1, N_EXPERTS), fixed)],
        out_shape=[jax.ShapeDtypeStruct((T, D_MODEL), F32),
                   jax.ShapeDtypeStruct((T, D_MODEL), F32),
                   jax.ShapeDtypeStruct((T, ROUTE_LANES), I32),
                   jax.ShapeDtypeStruct((T, ROUTE_LANES), F32),
                   jax.ShapeDtypeStruct((T, ROUTE_LANES), I32),
                   jax.ShapeDtypeStruct((1, N_EXPERTS), F32)],
        scratch_shapes=[pltpu.VMEM((1, N_EXPERTS), F32)],
        compiler_params=_cparams(("arbitrary",)),
        name="outproj_route",
    )(merged, x2, wo, g2, wr, br)


DISP_ROWS = 512
DMA_UNROLL = 8


def _dispatch_kernel(pos_ref, hn_ref, xs_ref, sem):
    def row_copy(g, tt, k):
        dst = pos_ref[0, 0, (g * DMA_UNROLL + tt) * TOP_K + k]
        return pltpu.make_async_copy(hn_ref.at[g, pl.ds(tt, 1), :], xs_ref.at[pl.ds(dst, 1), :], sem)

    def issue(g, carry):
        for tt in range(DMA_UNROLL):
            for k in range(TOP_K):
                row_copy(g, tt, k).start(priority=k % 2)
        return carry

    lax.fori_loop(0, DISP_ROWS // DMA_UNROLL, issue, 0)

    def drain(g, carry):
        for tt in range(DMA_UNROLL):
            for k in range(TOP_K):
                row_copy(g, tt, k).wait()
        return carry

    lax.fori_loop(0, DISP_ROWS // DMA_UNROLL, drain, 0)


def _dispatch(hn, pos, n_alloc):
    T = hn.shape[0]
    R = DISP_ROWS
    G = DMA_UNROLL
    pos3 = pos.reshape(T // R, 1, R * TOP_K)
    return pl.pallas_call(
        _dispatch_kernel,
        grid=(T // R,),
        in_specs=[pl.BlockSpec((1, 1, R * TOP_K), lambda i: (i, 0, 0), memory_space=pltpu.SMEM),
                  pl.BlockSpec((R // G, G, D_MODEL), lambda i: (i, 0, 0))],
        out_specs=pl.BlockSpec(memory_space=pl.ANY),
        out_shape=jax.ShapeDtypeStruct((n_alloc, D_MODEL), F32),
        scratch_shapes=[pltpu.SemaphoreType.DMA(())],
        compiler_params=_cparams(("arbitrary",)),
        name="dispatch",
    )(pos3, hn.reshape(T // G, G, D_MODEL))


def _expert_kernel(ue_ref, ur_ref, un_ref, xs_ref, wg_ref, wu_ref, wd_ref, bg_ref, bu_ref, bd_ref,
                   ys_ref, xb_ref, h_ref, stage_ref, ybuf_ref, pend_ref, xsem, ysem):
    u = pl.program_id(0)
    s = pl.program_id(1)
    nu = pl.num_programs(0)
    nblk = un_ref[u]
    row0 = ur_ref[u]
    slot = u % 2

    def blk_rows(b):
        return pl.ds(pl.multiple_of(b * ROW_BLK, ROW_BLK), ROW_BLK)

    def for_row_blocks(fn):
        def pair(i, carry):
            fn(pl.ds(pl.multiple_of(i * (2 * ROW_BLK), 2 * ROW_BLK), 2 * ROW_BLK))
            return carry

        lax.fori_loop(0, lax.shift_right_logical(nblk, 1), pair, 0)

        @pl.when((nblk & 1) == 1)
        def _():
            fn(blk_rows(nblk - 1))

    def x_copy(unit, b):
        rows = pl.ds(pl.multiple_of(ur_ref[unit] + b * ROW_BLK, ROW_BLK), ROW_BLK)
        return pltpu.make_async_copy(xs_ref.at[rows, :], stage_ref, xsem)

    def y_copy(b, n, ys):
        rows = pl.ds(pl.multiple_of(row0 + b * ROW_BLK, ROW_BLK), ROW_BLK)
        cols = pl.ds(pl.multiple_of(n * FF_TILE, FF_TILE), FF_TILE)
        return pltpu.make_async_copy(ybuf_ref.at[ys, blk_rows(b), :], ys_ref.at[rows, cols], ysem)

    def drain_stores():
        def wait_one(i, carry):
            y_copy(0, 0, 0).wait()
            return carry

        lax.fori_loop(0, pend_ref[0], wait_one, 0)
        pend_ref[0] = 0

    @pl.when((u == 0) & (s == 0))
    def _():
        pend_ref[0] = 0

        def load(b, carry):
            cp = x_copy(0, b)
            cp.start()
            cp.wait()
            xb_ref[0, blk_rows(b), :] = stage_ref[...].astype(BF16)
            return carry

        lax.fori_loop(0, nblk, load, 0)

    nxt = jnp.minimum(u + 1, nu - 1)
    prefetch = (u + 1 < nu) & (s < un_ref[nxt])

    @pl.when(prefetch)
    def _():
        x_copy(nxt, s).start()

    @pl.when(s < NJ)
    def _():
        wg = wg_ref[...].astype(BF16)
        wu = wu_ref[...].astype(BF16)
        bg = bg_ref[...]
        bu = bu_ref[...]

        def up_rows(rows):
            xb = xb_ref[slot, rows, :]
            g = jnp.minimum(jnp.dot(xb, wg, preferred_element_type=F32) + bg, SWIGLU_LIMIT)
            up = jnp.clip(jnp.dot(xb, wu, preferred_element_type=F32) + bu,
                          -SWIGLU_LIMIT, SWIGLU_LIMIT)
            h_ref[s, rows, :] = ((up + 1.0) * (g * jax.nn.sigmoid(SWIGLU_ALPHA * g))).astype(BF16)

        for_row_blocks(up_rows)

    @pl.when(s >= NJ)
    def _():
        n = s - NJ
        ys = n % 2
        wd = wd_ref[...].astype(BF16)
        bd = bd_ref[...]

        def down_rows(rows):
            hb = jnp.concatenate([h_ref[t, rows, :] for t in range(NJ)], axis=1)
            ybuf_ref[ys, rows, :] = jnp.dot(hb, wd, preferred_element_type=F32) + bd

        for_row_blocks(down_rows)
        drain_stores()

        def issue(b, carry):
            y_copy(b, n, ys).start()
            return carry

        lax.fori_loop(0, nblk, issue, 0)
        pend_ref[0] = nblk

    @pl.when(prefetch)
    def _():
        x_copy(nxt, s).wait()
        xb_ref[1 - slot, blk_rows(s), :] = stage_ref[...].astype(BF16)

    @pl.when((u == nu - 1) & (s == 2 * NJ - 1))
    def _():
        drain_stores()


def _experts(xs, unit_e, unit_row0, unit_nblk, n_live, w_gate, w_up, w_down, b_gate, b_up, b_down):
    n_alloc = xs.shape[0]
    n_units = unit_e.shape[0]
    last = NJ - 1

    def up_tile(u, s, ue, ur, un):
        return (ue[u], 0, jnp.where(un[u] > 0, jnp.minimum(s, last), last))

    def down_tile(u, s, ue, ur, un):
        return (ue[u], 0, jnp.where(un[u] > 0, jnp.maximum(s - NJ, 0), last))

    grid_spec = pltpu.PrefetchScalarGridSpec(
        num_scalar_prefetch=3,
        grid=(n_live, 2 * NJ),
        in_specs=[
            pl.BlockSpec(memory_space=pl.ANY),
            pl.BlockSpec((None, D_MODEL, FF_TILE), up_tile),
            pl.BlockSpec((None, D_MODEL, FF_TILE), up_tile),
            pl.BlockSpec((None, D_FF, FF_TILE), down_tile),
            pl.BlockSpec((None, 1, FF_TILE), up_tile),
            pl.BlockSpec((None, 1, FF_TILE), up_tile),
            pl.BlockSpec((None, 1, FF_TILE), down_tile),
        ],
        out_specs=pl.BlockSpec(memory_space=pl.ANY),
        scratch_shapes=[pltpu.VMEM((2, SEG_ROWS, D_MODEL), BF16),
                        pltpu.VMEM((NJ, SEG_ROWS, FF_TILE), BF16),
                        pltpu.VMEM((ROW_BLK, D_MODEL), F32),
                        pltpu.VMEM((2, SEG_ROWS, FF_TILE), F32),
                        pltpu.SMEM((1,), I32),
                        pltpu.SemaphoreType.DMA(()),
                        pltpu.SemaphoreType.DMA(())],
    )
    return pl.pallas_call(
        _expert_kernel,
        grid_spec=grid_spec,
        out_shape=jax.ShapeDtypeStruct((n_alloc, D_MODEL), F32),
        compiler_params=_cparams(("arbitrary", "arbitrary")),
        name="experts",
    )(unit_e, unit_row0, unit_nblk, xs, w_gate, w_up, w_down,
      b_gate.reshape(N_EXPERTS, 1, D_FF), b_up.reshape(N_EXPERTS, 1, D_FF),
      b_down.reshape(N_EXPERTS, 1, D_MODEL))


COMB_ROWS = 128


def _combine_kernel(pos_ref, nxt_ref, h1_ref, wt_ref, ys_ref, o_ref, buf_ref, sems):
    i = pl.program_id(0)
    n = pl.num_programs(0)
    slot = i % 2

    def row_copy(p_ref, s, g, tt, k):
        src = p_ref[0, 0, (g * DMA_UNROLL + tt) * TOP_K + k]
        return pltpu.make_async_copy(ys_ref.at[pl.ds(src, 1), :],
                                     buf_ref.at[s, k, g, pl.ds(tt, 1), :], sems.at[s])

    groups = COMB_ROWS // DMA_UNROLL

    def issue_group(p_ref, s, g):
        for tt in range(DMA_UNROLL):
            for k in range(TOP_K):
                row_copy(p_ref, s, g, tt, k).start(priority=k % 2)

    def accumulate_group(g):
        wt = wt_ref[g]
        acc = h1_ref[g]
        for k in range(TOP_K):
            acc = acc + wt[:, k:k + 1] * buf_ref[slot, k, g]
        o_ref[g] = acc

    def loop(body):
        def step(g, carry):
            body(g)
            return carry

        lax.fori_loop(0, groups, step, 0)

    @pl.when(i == 0)
    def _():
        loop(lambda g: issue_group(pos_ref, 0, g))

    def drain_group(g):
        for tt in range(DMA_UNROLL):
            for k in range(TOP_K):
                row_copy(pos_ref, slot, g, tt, k).wait()

    loop(drain_group)

    @pl.when(i + 1 < n)
    def _():
        def fused(g):
            issue_group(nxt_ref, 1 - slot, g)
            accumulate_group(g)

        loop(fused)

    @pl.when(i + 1 == n)
    def _():
        loop(accumulate_group)


def _combine(h1, wt, ys, pos):
    T = h1.shape[0]
    R = COMB_ROWS
    G = DMA_UNROLL
    nsteps = T // R
    pos3 = pos.reshape(nsteps, 1, R * TOP_K)
    tile = lambda w: pl.BlockSpec((R // G, G, w), lambda i: (i, 0, 0))
    out = pl.pallas_call(
        _combine_kernel,
        grid=(nsteps,),
        in_specs=[pl.BlockSpec((1, 1, R * TOP_K), lambda i: (i, 0, 0), memory_space=pltpu.SMEM),
                  pl.BlockSpec((1, 1, R * TOP_K), lambda i: (jnp.minimum(i + 1, nsteps - 1), 0, 0),
                               memory_space=pltpu.SMEM),
                  tile(D_MODEL),
                  tile(ROUTE_LANES),
                  pl.BlockSpec(memory_space=pl.ANY)],
        out_specs=tile(D_MODEL),
        out_shape=jax.ShapeDtypeStruct((T // G, G, D_MODEL), F32),
        scratch_shapes=[pltpu.VMEM((2, TOP_K, R // G, G, D_MODEL), F32),
                        pltpu.SemaphoreType.DMA((2,))],
        compiler_params=_cparams(("arbitrary",)),
        name="combine",
    )(pos3, pos3, h1.reshape(T // G, G, D_MODEL), wt.reshape(T // G, G, ROUTE_LANES), ys)
    return out.reshape(T, D_MODEL)


def _max_units(T):
    max_blocks = (T * TOP_K) // ROW_BLK + N_EXPERTS
    return (max_blocks + N_EXPERTS * (SEG_BLKS - 1)) // SEG_BLKS


def _routing_tables(counts, idx, rank, T):
    counts = counts.astype(I32)
    nblk = (counts + ROW_BLK - 1) // ROW_BLK
    pstart = (jnp.cumsum(nblk) - nblk) * ROW_BLK
    hot = idx[..., None] == jnp.arange(N_EXPERTS, dtype=I32)
    pos = jnp.sum(jnp.where(hot, pstart, 0), axis=-1) + rank
    units = (nblk + SEG_BLKS - 1) // SEG_BLKS
    uend = jnp.cumsum(units)
    ustart = uend - units
    n_units = _max_units(T)
    u = jnp.arange(n_units, dtype=I32)
    total = uend[-1]
    ue = jnp.minimum(jnp.searchsorted(uend, u, side="right").astype(I32), N_EXPERTS - 1)
    local = u - jnp.take(ustart, ue)
    un = jnp.clip(jnp.take(nblk, ue) - local * SEG_BLKS, 0, SEG_BLKS)
    ur = jnp.take(pstart, ue) + local * SEG_ROWS
    live = u < total
    last_e = jnp.take(ue, jnp.maximum(total - 1, 0))
    ue = jnp.where(live, ue, last_e)
    un = jnp.where(live, un, 0)
    ur = jnp.where(live, ur, 0)
    return pos.astype(I32), ue, ur.astype(I32), un.astype(I32), total.astype(I32)


def kernel(x, norm1_g, w_in, w_gla_gate_up, b_gla_gate, gla_norm_g, q_norm_g, k_norm_g, rel_bias,
           b_branch_gate, w_branch_gla, w_branch_att, w_out, norm2_g, w_router, b_router,
           w_gate, b_gate, w_up, b_up, w_down, b_down):
    B, S, D = x.shape
    T = B * S
    h = x.reshape(T, D)
    wt_in = jnp.swapaxes(w_in, 1, 2)
    for l in range(w_in.shape[0]):
        xn, glr = _norm1(h, norm1_g[l].reshape(1, D), wt_in[l, COL_LR:COL_B, :].astype(BF16))
        proj_a = _proj(xn, wt_in[l], 0, COLS_A, "inproj_gla")
        proj_b = _proj(xn, wt_in[l], COL_B, COLS_B, "inproj_att")
        y_gla = _gla(proj_a, glr, w_gla_gate_up[l], b_gla_gate[l].reshape(1, GLA_QK),
                     gla_norm_g[l].reshape(1, GLA_DV))
        y_att = _attention(proj_b, q_norm_g[l].reshape(1, ATT_HD), k_norm_g[l].reshape(1, ATT_HD),
                           _att_bias_diagonals(rel_bias[l]))
        merged = _merge(y_gla, y_att, w_branch_gla[l].astype(BF16), w_branch_att[l].astype(BF16),
                        proj_b, b_branch_gate[l])
        h1, hn, idx, wt, rank, counts = _route(
            merged, h, w_out[l].astype(BF16), norm2_g[l].reshape(1, D), w_router[l],
            b_router[l].reshape(1, N_EXPERTS))
        pos, ue, ur, un, n_live = _routing_tables(counts[0], idx[:, :TOP_K], rank[:, :TOP_K], T)
        n_alloc = ((T * TOP_K) // ROW_BLK + N_EXPERTS) * ROW_BLK
        xs = _dispatch(hn, pos.reshape(-1), n_alloc)
        ys = _experts(xs, ue, ur, un, n_live, w_gate[l], w_up[l], w_down[l], b_gate[l], b_up[l], b_down[l])
        h = _combine(h1, wt, ys, pos.reshape(-1))
    return h.reshape(B, S, D)
```

```python
import functools

import jax
import jax.numpy as jnp
import numpy as np
from jax import lax
from jax.experimental import pallas as pl
from jax.experimental.pallas import tpu as pltpu

F32 = jnp.float32
BF16 = jnp.bfloat16
I32 = jnp.int32

D_MODEL = 2048
CHUNK = 64
RMS_EPS = 1e-6
GLA_HEADS = 4
GLA_DK = 256
GLA_DV = 512
GLA_RANK = 16
GLA_TAU = 16.0
GLA_QK = GLA_HEADS * GLA_DK
GLA_V = GLA_HEADS * GLA_DV
ATT_HEADS = 8
ATT_HD = 128
ATT_W = ATT_HEADS * ATT_HD
ATT_LEFT = 8
REL_CLIP = 128
N_EXPERTS = 32
TOP_K = 4
D_FF = 2048
SWIGLU_LIMIT = 7.0
SWIGLU_ALPHA = 1.702

COLS_A = 2 * GLA_QK + 2 * GLA_V
COL_LR = COLS_A
COL_B = COLS_A + GLA_RANK
COLS_B = 3 * ATT_W + 2 * D_MODEL

VMEM_LIMIT = 56 * 1024 * 1024
NEG_BIG = -1e30

ROW_BLK = 256
SEG_BLKS = 5
SEG_ROWS = ROW_BLK * SEG_BLKS
FF_TILE = 512
NJ = D_FF // FF_TILE
assert D_MODEL // FF_TILE == NJ and SEG_BLKS <= 2 * NJ


def _cparams(sem):
    return pltpu.CompilerParams(dimension_semantics=sem, vmem_limit_bytes=VMEM_LIMIT)


def _norm1_kernel(x_ref, g_ref, wlr_ref, xn_ref, glr_ref):
    x = x_ref[...]
    ms = jnp.mean(x * x, axis=-1, keepdims=True)
    y = (x * lax.rsqrt(ms + RMS_EPS) * g_ref[...]).astype(BF16)
    xn_ref[...] = y
    glr_ref[...] = lax.dot_general(y, wlr_ref[...], (((1,), (1,)), ((), ())),
                                   preferred_element_type=F32)


def _norm1(x2, g, wlr):
    T = x2.shape[0]
    tm = 512
    return pl.pallas_call(
        _norm1_kernel,
        grid=(T // tm,),
        in_specs=[pl.BlockSpec((tm, D_MODEL), lambda i: (i, 0)),
                  pl.BlockSpec((1, D_MODEL), lambda i: (0, 0)),
                  pl.BlockSpec((GLA_RANK, D_MODEL), lambda i: (0, 0))],
        out_specs=[pl.BlockSpec((tm, D_MODEL), lambda i: (i, 0)),
                   pl.BlockSpec((tm, GLA_RANK), lambda i: (i, 0))],
        out_shape=[jax.ShapeDtypeStruct((T, D_MODEL), BF16),
                   jax.ShapeDtypeStruct((T, GLA_RANK), F32)],
        compiler_params=_cparams(("parallel",)),
        name="norm1",
    )(x2, g, wlr)


def _proj_kernel(x_ref, w_ref, o_ref, wb_ref):
    @pl.when(pl.program_id(1) == 0)
    def _():
        wb_ref[...] = w_ref[...].astype(BF16)

    o_ref[...] = lax.dot_general(x_ref[...], wb_ref[...], (((1,), (1,)), ((), ())),
                                 preferred_element_type=F32).astype(o_ref.dtype)


def _proj(xn, wt, col0, n_cols, name):
    T, K = xn.shape
    tm, tn = 1024, 1024
    assert col0 % 8 == 0
    return pl.pallas_call(
        _proj_kernel,
        grid=(n_cols // tn, T // tm),
        in_specs=[pl.BlockSpec((tm, K), lambda j, i: (i, 0)),
                  pl.BlockSpec((pl.Element(tn), pl.Element(K)),
                               lambda j, i: (pl.multiple_of(col0 + j * tn, 8), 0))],
        out_specs=pl.BlockSpec((tm, tn), lambda j, i: (i, j)),
        out_shape=jax.ShapeDtypeStruct((T, n_cols), BF16),
        scratch_shapes=[pltpu.VMEM((tn, K), BF16)],
        compiler_params=_cparams(("arbitrary", "arbitrary")),
        name=name,
    )(xn, wt)


GLA_ROWS = 256


def _gla_kernel(q_ref, k_ref, v_ref, r_ref, glr_ref, wup_ref, bg_ref, ng_ref, o_ref, st_ref):
    @pl.when(pl.program_id(0) == 0)
    def _():
        st_ref[...] = jnp.zeros_like(st_ref)

    z = jnp.dot(glr_ref[...], wup_ref[...], preferred_element_type=F32,
                precision=lax.Precision.HIGHEST) + bg_ref[...]
    log_a = (jnp.minimum(z, 0.0) - jnp.log(1.0 + jnp.exp(-jnp.abs(z)))) * (1.0 / GLA_TAU)

    ri = lax.broadcasted_iota(I32, (CHUNK, CHUNK), 0)
    ci = lax.broadcasted_iota(I32, (CHUNK, CHUNK), 1)
    causal = ci <= ri
    tril = jnp.where(causal, 1.0, 0.0).astype(BF16)
    scale = GLA_DK ** -0.5
    ng = ng_ref[...]

    for c in range(GLA_ROWS // CHUNK):
        rows = slice(c * CHUNK, (c + 1) * CHUNK)
        la = log_a[rows]
        la_hi = la.astype(BF16)
        la_lo = (la - la_hi.astype(F32)).astype(BF16)
        bcum_all = (jnp.dot(tril, la_hi, preferred_element_type=F32)
                    + jnp.dot(tril, la_lo, preferred_element_type=F32))
        for h in range(GLA_HEADS):
            kcols = slice(h * GLA_DK, (h + 1) * GLA_DK)
            vcols = slice(h * GLA_DV, (h + 1) * GLA_DV)
            bcum = bcum_all[:, kcols]
            blast = bcum[CHUNK - 1:CHUNK, :]
            qc = q_ref[rows, kcols].astype(F32) * scale
            kc = k_ref[rows, kcols].astype(F32)
            q_dec = (qc * jnp.exp(bcum)).astype(BF16)
            k_inv = (kc * jnp.exp(-bcum)).astype(BF16)
            k_end = (kc * jnp.exp(blast - bcum)).astype(BF16)
            vv = v_ref[rows, vcols]
            a = lax.dot_general(q_dec, k_inv, (((1,), (1,)), ((), ())), preferred_element_type=F32)
            a = jnp.where(causal, a, 0.0).astype(BF16)
            st = st_ref[h]
            o = jnp.dot(a, vv, preferred_element_type=F32)
            o = o + lax.dot_general(q_dec, st.astype(BF16), (((1,), (1,)), ((), ())),
                                    preferred_element_type=F32)
            st_ref[h] = st * jnp.exp(blast) + lax.dot_general(
                vv, k_end, (((0,), (0,)), ((), ())), preferred_element_type=F32)
            ms = jnp.mean(o * o, axis=-1, keepdims=True)
            y = o * lax.rsqrt(ms + RMS_EPS) * ng
            rr = r_ref[rows, vcols].astype(F32)
            o_ref[rows, vcols] = (y * (rr * jax.nn.sigmoid(rr))).astype(o_ref.dtype)


def _gla(proj_a, glr, wup, bg, ng):
    T = proj_a.shape[0]
    R = GLA_ROWS
    return pl.pallas_call(
        _gla_kernel,
        grid=(T // R,),
        in_specs=[pl.BlockSpec((R, GLA_QK), lambda n: (n, 0)),
                  pl.BlockSpec((R, GLA_QK), lambda n: (n, 1)),
                  pl.BlockSpec((R, GLA_V), lambda n: (n, 1)),
                  pl.BlockSpec((R, GLA_V), lambda n: (n, 2)),
                  pl.BlockSpec((R, GLA_RANK), lambda n: (n, 0)),
                  pl.BlockSpec((GLA_RANK, GLA_QK), lambda n: (0, 0)),
                  pl.BlockSpec((1, GLA_QK), lambda n: (0, 0)),
                  pl.BlockSpec((1, GLA_DV), lambda n: (0, 0))],
        out_specs=pl.BlockSpec((R, GLA_V), lambda n: (n, 0)),
        out_shape=jax.ShapeDtypeStruct((T, GLA_V), BF16),
        scratch_shapes=[pltpu.VMEM((GLA_HEADS, GLA_DV, GLA_DK), F32)],
        compiler_params=_cparams(("arbitrary",)),
        name="gla",
    )(proj_a, proj_a, proj_a, proj_a, glr, wup, bg, ng)


ATT_QROWS = 256
ATT_PAD = ATT_LEFT * CHUNK
ATT_BAND = ATT_PAD + ATT_QROWS


ATT_EXT = 1024
ATT_GROUP = 2


def _att_bias_diagonals(rel_bias):
    m = np.arange(ATT_EXT)
    m = np.where(m < ATT_BAND, m, m - ATT_EXT)
    rel = np.clip(ATT_PAD - m, -REL_CLIP, REL_CLIP) + REL_CLIP
    return rel_bias.astype(F32)[:, rel].reshape(ATT_HEADS, 1, ATT_EXT)


def _att_kernel(q_ref, k_ref, v_ref, qg_ref, kg_ref, ext_ref, o_ref, kn_ref, vp_ref, tab_ref):
    qb = pl.program_id(1)

    @pl.when(qb == 0)
    def _():
        qc = lax.broadcasted_iota(I32, (ATT_QROWS, ATT_BAND), 0) // CHUNK
        kc = lax.broadcasted_iota(I32, (ATT_QROWS, ATT_BAND), 1) // CHUNK
        in_band = (kc >= qc) & (kc <= qc + ATT_LEFT)
        for g in range(ATT_GROUP):
            cols = slice(g * ATT_HD, (g + 1) * ATT_HD)
            ext = jnp.broadcast_to(ext_ref[g], (ATT_QROWS, ATT_EXT))
            tab = pltpu.roll(ext, 0, 1, stride=1, stride_axis=0)[:, :ATT_BAND]
            tab_ref[g] = jnp.where(in_band, tab, NEG_BIG)
            kk = k_ref[:, cols].astype(F32)
            ms = jnp.mean(kk * kk, axis=-1, keepdims=True)
            kn = kk * lax.rsqrt(ms + RMS_EPS) * kg_ref[...]
            kn_ref[g, 0:ATT_PAD, :] = jnp.zeros((ATT_PAD, ATT_HD), BF16)
            kn_ref[g, ATT_PAD:, :] = kn.astype(BF16)
            vp_ref[g, 0:ATT_PAD, :] = jnp.zeros((ATT_PAD, ATT_HD), BF16)
            vp_ref[g, ATT_PAD:, :] = v_ref[:, cols]

    start = pl.multiple_of(qb * ATT_QROWS, ATT_QROWS)
    kpos = start - ATT_PAD + lax.broadcasted_iota(I32, (1, ATT_BAND), 1)
    for g in range(ATT_GROUP):
        cols = slice(g * ATT_HD, (g + 1) * ATT_HD)
        q = q_ref[:, cols].astype(F32)
        ms = jnp.mean(q * q, axis=-1, keepdims=True)
        qn = (q * lax.rsqrt(ms + RMS_EPS) * qg_ref[...] * (ATT_HD ** -0.5)).astype(BF16)
        kb = kn_ref[g, pl.ds(start, ATT_BAND), :]
        vb = vp_ref[g, pl.ds(start, ATT_BAND), :]
        s = lax.dot_general(qn, kb, (((1,), (1,)), ((), ())), preferred_element_type=F32)
        s = s + tab_ref[g]
        s = jnp.where(kpos >= 0, s, NEG_BIG)
        m = jnp.max(s, axis=-1, keepdims=True)
        p = jnp.exp(s - m)
        l = jnp.sum(p, axis=-1, keepdims=True)
        o = jnp.dot(p.astype(BF16), vb, preferred_element_type=F32)
        o_ref[:, cols] = (o / l).astype(o_ref.dtype)


def _attention(proj_b, qg, kg, ext):
    T = proj_b.shape[0]
    R = ATT_QROWS
    GW = ATT_GROUP * ATT_HD
    NG = ATT_HEADS // ATT_GROUP
    return pl.pallas_call(
        _att_kernel,
        grid=(NG, T // R),
        in_specs=[pl.BlockSpec((R, GW), lambda h, n: (n, h)),
                  pl.BlockSpec((T, GW), lambda h, n: (0, NG + h)),
                  pl.BlockSpec((T, GW), lambda h, n: (0, 2 * NG + h)),
                  pl.BlockSpec((1, ATT_HD), lambda h, n: (0, 0)),
                  pl.BlockSpec((1, ATT_HD), lambda h, n: (0, 0)),
                  pl.BlockSpec((ATT_GROUP, 1, ATT_EXT), lambda h, n: (h, 0, 0))],
        out_specs=pl.BlockSpec((R, GW), lambda h, n: (n, h)),
        out_shape=jax.ShapeDtypeStruct((T, ATT_W), BF16),
        scratch_shapes=[pltpu.VMEM((ATT_GROUP, T + ATT_PAD, ATT_HD), BF16),
                        pltpu.VMEM((ATT_GROUP, T + ATT_PAD, ATT_HD), BF16),
                        pltpu.VMEM((ATT_GROUP, R, ATT_BAND), F32)],
        compiler_params=_cparams(("arbitrary", "arbitrary")),
        name="chunk_attention",
    )(proj_b, proj_b, proj_b, qg, kg, ext)


def _merge_kernel(yg_ref, ya_ref, wg_ref, wa_ref, gg_ref, ga_ref, bb_ref, o_ref):
    pg = jnp.dot(yg_ref[...], wg_ref[...], preferred_element_type=F32)
    pa = jnp.dot(ya_ref[...], wa_ref[...], preferred_element_type=F32)
    bb = bb_ref[...]
    sg = jax.nn.sigmoid(gg_ref[...].astype(F32) + bb[0:1, :])
    sa = jax.nn.sigmoid(ga_ref[...].astype(F32) + bb[1:2, :])
    o_ref[...] = (sg * pg + sa * pa).astype(o_ref.dtype)


def _merge(y_gla, y_att, wg, wa, proj_b, bb):
    T = y_gla.shape[0]
    tm, tn = 1024, 1024
    gate0 = 3 * ATT_W // tn
    return pl.pallas_call(
        _merge_kernel,
        grid=(D_MODEL // tn, T // tm),
        in_specs=[pl.BlockSpec((tm, GLA_V), lambda j, i: (i, 0)),
                  pl.BlockSpec((tm, ATT_W), lambda j, i: (i, 0)),
                  pl.BlockSpec((GLA_V, tn), lambda j, i: (0, j)),
                  pl.BlockSpec((ATT_W, tn), lambda j, i: (0, j)),
                  pl.BlockSpec((tm, tn), lambda j, i: (i, gate0 + j)),
                  pl.BlockSpec((tm, tn), lambda j, i: (i, gate0 + D_MODEL // tn + j)),
                  pl.BlockSpec((2, tn), lambda j, i: (0, j))],
        out_specs=pl.BlockSpec((tm, tn), lambda j, i: (i, j)),
        out_shape=jax.ShapeDtypeStruct((T, D_MODEL), BF16),
        compiler_params=_cparams(("arbitrary", "arbitrary")),
        name="merge",
    )(y_gla, y_att, wg, wa, proj_b, proj_b, bb)


ROUTE_ROWS = 512
ROUTE_SUB = 256
ROUTE_LANES = 128


def _route_kernel(m_ref, x_ref, wo_ref, g2_ref, wr_ref, br_ref,
                  h1_ref, hn_ref, idx_ref, wt_ref, rank_ref, cnt_ref, run_ref):
    @pl.when(pl.program_id(0) == 0)
    def _():
        run_ref[...] = jnp.zeros_like(run_ref)

    run = run_ref[...]
    for sub in range(ROUTE_ROWS // ROUTE_SUB):
        rows = slice(sub * ROUTE_SUB, (sub + 1) * ROUTE_SUB)
        h1 = x_ref[rows, :] + jnp.dot(m_ref[rows, :], wo_ref[...], preferred_element_type=F32)
        h1_ref[rows, :] = h1
        ms = jnp.mean(h1 * h1, axis=-1, keepdims=True)
        hn = h1 * lax.rsqrt(ms + RMS_EPS) * g2_ref[...]
        hn_ref[rows, :] = hn
        wr = wr_ref[...]
        hn_hi = hn.astype(BF16)
        hn_lo = (hn - hn_hi.astype(F32)).astype(BF16)
        wr_hi = wr.astype(BF16)
        wr_lo = (wr - wr_hi.astype(F32)).astype(BF16)
        logits = (jnp.dot(hn_hi, wr_hi, preferred_element_type=F32)
                  + jnp.dot(hn_hi, wr_lo, preferred_element_type=F32)
                  + jnp.dot(hn_lo, wr_hi, preferred_element_type=F32)) + br_ref[...]

        R = ROUTE_SUB
        lanes = lax.broadcasted_iota(I32, (R, N_EXPERTS), 1)
        work = logits
        vals, sels, idxs = [], [], []
        for _ in range(TOP_K):
            m = jnp.max(work, axis=-1, keepdims=True)
            idx = jnp.min(jnp.where(work == m, lanes, N_EXPERTS), axis=-1, keepdims=True)
            sel = lanes == idx
            vals.append(m)
            idxs.append(idx)
            sels.append(sel)
            work = jnp.where(sel, -jnp.inf, work)
        es = [jnp.exp(v - vals[0]) for v in vals]
        denom = es[0] + es[1] + es[2] + es[3]

        hot = jnp.zeros((R, N_EXPERTS), F32)
        for sel in sels:
            hot = jnp.where(sel, 1.0, hot)
        ri = lax.broadcasted_iota(I32, (R, R), 0)
        ci = lax.broadcasted_iota(I32, (R, R), 1)
        strict = jnp.where(ci < ri, 1.0, 0.0).astype(BF16)
        before = jnp.dot(strict, hot.astype(BF16), preferred_element_type=F32) + run
        run = run + jnp.sum(hot, axis=0, keepdims=True)

        ol = lax.broadcasted_iota(I32, (R, ROUTE_LANES), 1)
        idx_o = jnp.zeros((R, ROUTE_LANES), I32)
        wt_o = jnp.zeros((R, ROUTE_LANES), F32)
        rank_o = jnp.zeros((R, ROUTE_LANES), I32)
        for k in range(TOP_K):
            rk = jnp.sum(jnp.where(sels[k], before, 0.0), axis=-1, keepdims=True).astype(I32)
            idx_o = jnp.where(ol == k, idxs[k], idx_o)
            wt_o = jnp.where(ol == k, es[k] / denom, wt_o)
            rank_o = jnp.where(ol == k, rk, rank_o)
        idx_ref[rows, :] = idx_o
        wt_ref[rows, :] = wt_o
        rank_ref[rows, :] = rank_o
    run_ref[...] = run
    cnt_ref[...] = run


def _route(merged, x2, wo, g2, wr, br):
    T = x2.shape[0]
    R = ROUTE_ROWS
    row = lambda i: (i, 0)
    fixed = lambda i: (0, 0)
    return pl.pallas_call(
        _route_kernel,
        grid=(T // R,),
        in_specs=[pl.BlockSpec((R, D_MODEL), row),
                  pl.BlockSpec((R, D_MODEL), row),
                  pl.BlockSpec((D_MODEL, D_MODEL), fixed),
                  pl.BlockSpec((1, D_MODEL), fixed),
                  pl.BlockSpec((D_MODEL, N_EXPERTS), fixed),
                  pl.BlockSpec((1, N_EXPERTS), fixed)],
        out_specs=[pl.BlockSpec((R, D_MODEL), row),
                   pl.BlockSpec((R, D_MODEL), row),
                   pl.BlockSpec((R, ROUTE_LANES), row),
                   pl.BlockSpec((R, ROUTE_LANES), row),
                   pl.BlockSpec((R, ROUTE_LANES), row),
                   pl.BlockSpec((1, N_EXPERTS), fixed)],
        out_shape=[jax.ShapeDtypeStruct((T, D_MODEL), F32),
                   jax.ShapeDtypeStruct((T, D_MODEL), F32),
                   jax.ShapeDtypeStruct((T, ROUTE_LANES), I32),
                   jax.ShapeDtypeStruct((T, ROUTE_LANES), F32),
                   jax.ShapeDtypeStruct((T, ROUTE_LANES), I32),
                   jax.ShapeDtypeStruct((1, N_EXPERTS), F32)],
        scratch_shapes=[pltpu.VMEM((1, N_EXPERTS), F32)],
        compiler_params=_cparams(("arbitrary",)),
        name="outproj_route",
    )(merged, x2, wo, g2, wr, br)


DISP_ROWS = 512
DMA_UNROLL = 8


def _dispatch_kernel(pos_ref, hn_ref, xs_ref, sem):
    def row_copy(g, tt, k):
        dst = pos_ref[0, 0, (g * DMA_UNROLL + tt) * TOP_K + k]
        return pltpu.make_async_copy(hn_ref.at[g, pl.ds(tt, 1), :], xs_ref.at[pl.ds(dst, 1), :], sem)

    def issue(g, carry):
        for tt in range(DMA_UNROLL):
            for k in range(TOP_K):
                row_copy(g, tt, k).start(priority=k % 2)
        return carry

    lax.fori_loop(0, DISP_ROWS // DMA_UNROLL, issue, 0)

    def drain(g, carry):
        for tt in range(DMA_UNROLL):
            for k in range(TOP_K):
                row_copy(g, tt, k).wait()
        return carry

    lax.fori_loop(0, DISP_ROWS // DMA_UNROLL, drain, 0)


def _dispatch(hn, pos, n_alloc):
    T = hn.shape[0]
    R = DISP_ROWS
    G = DMA_UNROLL
    pos3 = pos.reshape(T // R, 1, R * TOP_K)
    return pl.pallas_call(
        _dispatch_kernel,
        grid=(T // R,),
        in_specs=[pl.BlockSpec((1, 1, R * TOP_K), lambda i: (i, 0, 0), memory_space=pltpu.SMEM),
                  pl.BlockSpec((R // G, G, D_MODEL), lambda i: (i, 0, 0))],
        out_specs=pl.BlockSpec(memory_space=pl.ANY),
        out_shape=jax.ShapeDtypeStruct((n_alloc, D_MODEL), F32),
        scratch_shapes=[pltpu.SemaphoreType.DMA(())],
        compiler_params=_cparams(("arbitrary",)),
        name="dispatch",
    )(pos3, hn.reshape(T // G, G, D_MODEL))


def _expert_kernel(ue_ref, ur_ref, un_ref, xs_ref, wg_ref, wu_ref, wd_ref, bg_ref, bu_ref, bd_ref,
                   ys_ref, xb_ref, h_ref, stage_ref, ybuf_ref, pend_ref, xsem, ysem):
    u = pl.program_id(0)
    s = pl.program_id(1)
    nu = pl.num_programs(0)
    nblk = un_ref[u]
    row0 = ur_ref[u]
    slot = u % 2

    def blk_rows(b):
        return pl.ds(pl.multiple_of(b * ROW_BLK, ROW_BLK), ROW_BLK)

    def for_row_blocks(fn):
        def pair(i, carry):
            fn(pl.ds(pl.multiple_of(i * (2 * ROW_BLK), 2 * ROW_BLK), 2 * ROW_BLK))
            return carry

        lax.fori_loop(0, lax.shift_right_logical(nblk, 1), pair, 0)

        @pl.when((nblk & 1) == 1)
        def _():
            fn(blk_rows(nblk - 1))

    def x_copy(unit, b):
        rows = pl.ds(pl.multiple_of(ur_ref[unit] + b * ROW_BLK, ROW_BLK), ROW_BLK)
        return pltpu.make_async_copy(xs_ref.at[rows, :], stage_ref, xsem)

    def y_copy(b, n, ys):
        rows = pl.ds(pl.multiple_of(row0 + b * ROW_BLK, ROW_BLK), ROW_BLK)
        cols = pl.ds(pl.multiple_of(n * FF_TILE, FF_TILE), FF_TILE)
        return pltpu.make_async_copy(ybuf_ref.at[ys, blk_rows(b), :], ys_ref.at[rows, cols], ysem)

    def drain_stores():
        def wait_one(i, carry):
            y_copy(0, 0, 0).wait()
            return carry

        lax.fori_loop(0, pend_ref[0], wait_one, 0)
        pend_ref[0] = 0

    @pl.when((u == 0) & (s == 0))
    def _():
        pend_ref[0] = 0

        def load(b, carry):
            cp = x_copy(0, b)
            cp.start()
            cp.wait()
            xb_ref[0, blk_rows(b), :] = stage_ref[...].astype(BF16)
            return carry

        lax.fori_loop(0, nblk, load, 0)

    nxt = jnp.minimum(u + 1, nu - 1)
    prefetch = (u + 1 < nu) & (s < un_ref[nxt])

    @pl.when(prefetch)
    def _():
        x_copy(nxt, s).start()

    @pl.when(s < NJ)
    def _():
        wg = wg_ref[...].astype(BF16)
        wu = wu_ref[...].astype(BF16)
        bg = bg_ref[...]
        bu = bu_ref[...]

        def up_rows(rows):
            xb = xb_ref[slot, rows, :]
            g = jnp.minimum(jnp.dot(xb, wg, preferred_element_type=F32) + bg, SWIGLU_LIMIT)
            up = jnp.clip(jnp.dot(xb, wu, preferred_element_type=F32) + bu,
                          -SWIGLU_LIMIT, SWIGLU_LIMIT)
            h_ref[s, rows, :] = ((up + 1.0) * (g * jax.nn.sigmoid(SWIGLU_ALPHA * g))).astype(BF16)

        for_row_blocks(up_rows)

    @pl.when(s >= NJ)
    def _():
        n = s - NJ
        ys = n % 2
        wd = wd_ref[...].astype(BF16)
        bd = bd_ref[...]

        def down_rows(rows):
            hb = jnp.concatenate([h_ref[t, rows, :] for t in range(NJ)], axis=1)
            ybuf_ref[ys, rows, :] = jnp.dot(hb, wd, preferred_element_type=F32) + bd

        for_row_blocks(down_rows)
        drain_stores()

        def issue(b, carry):
            y_copy(b, n, ys).start()
            return carry

        lax.fori_loop(0, nblk, issue, 0)
        pend_ref[0] = nblk

    @pl.when(prefetch)
    def _():
        x_copy(nxt, s).wait()
        xb_ref[1 - slot, blk_rows(s), :] = stage_ref[...].astype(BF16)

    @pl.when((u == nu - 1) & (s == 2 * NJ - 1))
    def _():
        drain_stores()


def _experts(xs, unit_e, unit_row0, unit_nblk, n_live, w_gate, w_up, w_down, b_gate, b_up, b_down):
    n_alloc = xs.shape[0]
    n_units = unit_e.shape[0]
    last = NJ - 1

    def up_tile(u, s, ue, ur, un):
        return (ue[u], 0, jnp.where(un[u] > 0, jnp.minimum(s, last), last))

    def down_tile(u, s, ue, ur, un):
        return (ue[u], 0, jnp.where(un[u] > 0, jnp.maximum(s - NJ, 0), last))

    grid_spec = pltpu.PrefetchScalarGridSpec(
        num_scalar_prefetch=3,
        grid=(n_live, 2 * NJ),
        in_specs=[
            pl.BlockSpec(memory_space=pl.ANY),
            pl.BlockSpec((None, D_MODEL, FF_TILE), up_tile),
            pl.BlockSpec((None, D_MODEL, FF_TILE), up_tile),
            pl.BlockSpec((None, D_FF, FF_TILE), down_tile),
            pl.BlockSpec((None, 1, FF_TILE), up_tile),
            pl.BlockSpec((None, 1, FF_TILE), up_tile),
            pl.BlockSpec((None, 1, FF_TILE), down_tile),
        ],
        out_specs=pl.BlockSpec(memory_space=pl.ANY),
        scratch_shapes=[pltpu.VMEM((2, SEG_ROWS, D_MODEL), BF16),
                        pltpu.VMEM((NJ, SEG_ROWS, FF_TILE), BF16),
                        pltpu.VMEM((ROW_BLK, D_MODEL), F32),
                        pltpu.VMEM((2, SEG_ROWS, FF_TILE), F32),
                        pltpu.SMEM((1,), I32),
                        pltpu.SemaphoreType.DMA(()),
                        pltpu.SemaphoreType.DMA(())],
    )
    return pl.pallas_call(
        _expert_kernel,
        grid_spec=grid_spec,
        out_shape=jax.ShapeDtypeStruct((n_alloc, D_MODEL), F32),
        compiler_params=_cparams(("arbitrary", "arbitrary")),
        name="experts",
    )(unit_e, unit_row0, unit_nblk, xs, w_gate, w_up, w_down,
      b_gate.reshape(N_EXPERTS, 1, D_FF), b_up.reshape(N_EXPERTS, 1, D_FF),
      b_down.reshape(N_EXPERTS, 1, D_MODEL))


COMB_ROWS = 128


def _combine_kernel(pos_ref, nxt_ref, h1_ref, wt_ref, ys_ref, o_ref, buf_ref, sems):
    i = pl.program_id(0)
    n = pl.num_programs(0)
    slot = i % 2

    def row_copy(p_ref, s, g, tt, k):
        src = p_ref[0, 0, (g * DMA_UNROLL + tt) * TOP_K + k]
        return pltpu.make_async_copy(ys_ref.at[pl.ds(src, 1), :],
                                     buf_ref.at[s, k, g, pl.ds(tt, 1), :], sems.at[s])

    def issue_all(p_ref, s):
        def issue(g, carry):
            for tt in range(DMA_UNROLL):
                for k in range(TOP_K):
                    row_copy(p_ref, s, g, tt, k).start(priority=k % 2)
            return carry

        lax.fori_loop(0, COMB_ROWS // DMA_UNROLL, issue, 0)

    @pl.when(i == 0)
    def _():
        issue_all(pos_ref, 0)

    @pl.when(i + 1 < n)
    def _():
        issue_all(nxt_ref, 1 - slot)

    def drain(g, carry):
        for tt in range(DMA_UNROLL):
            for k in range(TOP_K):
                row_copy(pos_ref, slot, g, tt, k).wait()
        return carry

    lax.fori_loop(0, COMB_ROWS // DMA_UNROLL, drain, 0)

    wt = wt_ref[...]
    acc = h1_ref[...]
    for k in range(TOP_K):
        acc = acc + wt[:, :, k:k + 1] * buf_ref[slot, k]
    o_ref[...] = acc


def _combine(h1, wt, ys, pos):
    T = h1.shape[0]
    R = COMB_ROWS
    G = DMA_UNROLL
    nsteps = T // R
    pos3 = pos.reshape(nsteps, 1, R * TOP_K)
    tile = lambda w: pl.BlockSpec((R // G, G, w), lambda i: (i, 0, 0))
    out = pl.pallas_call(
        _combine_kernel,
        grid=(nsteps,),
        in_specs=[pl.BlockSpec((1, 1, R * TOP_K), lambda i: (i, 0, 0), memory_space=pltpu.SMEM),
                  pl.BlockSpec((1, 1, R * TOP_K), lambda i: (jnp.minimum(i + 1, nsteps - 1), 0, 0),
                               memory_space=pltpu.SMEM),
                  tile(D_MODEL),
                  tile(ROUTE_LANES),
                  pl.BlockSpec(memory_space=pl.ANY)],
        out_specs=tile(D_MODEL),
        out_shape=jax.ShapeDtypeStruct((T // G, G, D_MODEL), F32),
        scratch_shapes=[pltpu.VMEM((2, TOP_K, R // G, G, D_MODEL), F32),
                        pltpu.SemaphoreType.DMA((2,))],
        compiler_params=_cparams(("arbitrary",)),
        name="combine",
    )(pos3, pos3, h1.reshape(T // G, G, D_MODEL), wt.reshape(T // G, G, ROUTE_LANES), ys)
    return out.reshape(T, D_MODEL)


def _max_units(T):
    max_blocks = (T * TOP_K) // ROW_BLK + N_EXPERTS
    return (max_blocks + N_EXPERTS * (SEG_BLKS - 1)) // SEG_BLKS


def _routing_tables(counts, idx, rank, T):
    counts = counts.astype(I32)
    nblk = (counts + ROW_BLK - 1) // ROW_BLK
    pstart = (jnp.cumsum(nblk) - nblk) * ROW_BLK
    hot = idx[..., None] == jnp.arange(N_EXPERTS, dtype=I32)
    pos = jnp.sum(jnp.where(hot, pstart, 0), axis=-1) + rank
    units = (nblk + SEG_BLKS - 1) // SEG_BLKS
    uend = jnp.cumsum(units)
    ustart = uend - units
    n_units = _max_units(T)
    u = jnp.arange(n_units, dtype=I32)
    total = uend[-1]
    ue = jnp.minimum(jnp.searchsorted(uend, u, side="right").astype(I32), N_EXPERTS - 1)
    local = u - jnp.take(ustart, ue)
    un = jnp.clip(jnp.take(nblk, ue) - local * SEG_BLKS, 0, SEG_BLKS)
    ur = jnp.take(pstart, ue) + local * SEG_ROWS
    live = u < total
    last_e = jnp.take(ue, jnp.maximum(total - 1, 0))
    ue = jnp.where(live, ue, last_e)
    un = jnp.where(live, un, 0)
    ur = jnp.where(live, ur, 0)
    return pos.astype(I32), ue, ur.astype(I32), un.astype(I32), total.astype(I32)


def kernel(x, norm1_g, w_in, w_gla_gate_up, b_gla_gate, gla_norm_g, q_norm_g, k_norm_g, rel_bias,
           b_branch_gate, w_branch_gla, w_branch_att, w_out, norm2_g, w_router, b_router,
           w_gate, b_gate, w_up, b_up, w_down, b_down):
    B, S, D = x.shape
    T = B * S
    h = x.reshape(T, D)
    wt_in = jnp.swapaxes(w_in, 1, 2)
    for l in range(w_in.shape[0]):
        xn, glr = _norm1(h, norm1_g[l].reshape(1, D), wt_in[l, COL_LR:COL_B, :].astype(BF16))
        proj_a = _proj(xn, wt_in[l], 0, COLS_A, "inproj_gla")
        proj_b = _proj(xn, wt_in[l], COL_B, COLS_B, "inproj_att")
        y_gla = _gla(proj_a, glr, w_gla_gate_up[l], b_gla_gate[l].reshape(1, GLA_QK),
                     gla_norm_g[l].reshape(1, GLA_DV))
        y_att = _attention(proj_b, q_norm_g[l].reshape(1, ATT_HD), k_norm_g[l].reshape(1, ATT_HD),
                           _att_bias_diagonals(rel_bias[l]))
        merged = _merge(y_gla, y_att, w_branch_gla[l].astype(BF16), w_branch_att[l].astype(BF16),
                        proj_b, b_branch_gate[l])
        h1, hn, idx, wt, rank, counts = _route(
            merged, h, w_out[l].astype(BF16), norm2_g[l].reshape(1, D), w_router[l],
            b_router[l].reshape(1, N_EXPERTS))
        pos, ue, ur, un, n_live = _routing_tables(counts[0], idx[:, :TOP_K], rank[:, :TOP_K], T)
        n_alloc = ((T * TOP_K) // ROW_BLK + N_EXPERTS) * ROW_BLK
        xs = _dispatch(hn, pos.reshape(-1), n_alloc)
        ys = _experts(xs, ue, ur, un, n_live, w_gate[l], w_up[l], w_down[l], b_gate[l], b_up[l], b_down[l])
        h = _combine(h1, wt, ys, pos.reshape(-1))
    return h.reshape(B, S, D)
```

```python
import functools

import jax
import jax.numpy as jnp
import numpy as np
from jax import lax
from jax.experimental import pallas as pl
from jax.experimental.pallas import tpu as pltpu

F32 = jnp.float32
BF16 = jnp.bfloat16
I32 = jnp.int32

D_MODEL = 2048
CHUNK = 64
RMS_EPS = 1e-6
GLA_HEADS = 4
GLA_DK = 256
GLA_DV = 512
GLA_RANK = 16
GLA_TAU = 16.0
GLA_QK = GLA_HEADS * GLA_DK
GLA_V = GLA_HEADS * GLA_DV
ATT_HEADS = 8
ATT_HD = 128
ATT_W = ATT_HEADS * ATT_HD
ATT_LEFT = 8
REL_CLIP = 128
N_EXPERTS = 32
TOP_K = 4
D_FF = 2048
SWIGLU_LIMIT = 7.0
SWIGLU_ALPHA = 1.702

COLS_A = 2 * GLA_QK + 2 * GLA_V
COL_LR = COLS_A
COL_B = COLS_A + GLA_RANK
COLS_B = 3 * ATT_W + 2 * D_MODEL

V7X_VMEM_BYTES = 64 * 1024 * 1024
VMEM_LIMIT = V7X_VMEM_BYTES * 7 // 8
NEG_BIG = -1e30

ROW_BLK = 256
SEG_BLKS = 5
SEG_ROWS = ROW_BLK * SEG_BLKS
FF_TILE = 512
NJ = D_FF // FF_TILE
assert D_MODEL // FF_TILE == NJ and SEG_BLKS <= 2 * NJ


def _cparams(sem):
    return pltpu.CompilerParams(dimension_semantics=sem, vmem_limit_bytes=VMEM_LIMIT)


def _norm1_kernel(x_ref, g_ref, wlr_ref, xn_ref, glr_ref):
    x = x_ref[...]
    ms = jnp.mean(x * x, axis=-1, keepdims=True)
    y = (x * lax.rsqrt(ms + RMS_EPS) * g_ref[...]).astype(BF16)
    xn_ref[...] = y
    glr_ref[...] = lax.dot_general(y, wlr_ref[...], (((1,), (1,)), ((), ())),
                                   preferred_element_type=F32)


def _norm1(x2, g, wlr):
    T = x2.shape[0]
    tm = 512
    return pl.pallas_call(
        _norm1_kernel,
        grid=(T // tm,),
        in_specs=[pl.BlockSpec((tm, D_MODEL), lambda i: (i, 0)),
                  pl.BlockSpec((1, D_MODEL), lambda i: (0, 0)),
                  pl.BlockSpec((GLA_RANK, D_MODEL), lambda i: (0, 0))],
        out_specs=[pl.BlockSpec((tm, D_MODEL), lambda i: (i, 0)),
                   pl.BlockSpec((tm, GLA_RANK), lambda i: (i, 0))],
        out_shape=[jax.ShapeDtypeStruct((T, D_MODEL), BF16),
                   jax.ShapeDtypeStruct((T, GLA_RANK), F32)],
        compiler_params=_cparams(("parallel",)),
        name="norm1",
    )(x2, g, wlr)


def _proj_kernel(x_ref, w_ref, o_ref, wb_ref):
    @pl.when(pl.program_id(1) == 0)
    def _():
        wb_ref[...] = w_ref[...].astype(BF16)

    o_ref[...] = lax.dot_general(x_ref[...], wb_ref[...], (((1,), (1,)), ((), ())),
                                 preferred_element_type=F32).astype(o_ref.dtype)


def _proj(xn, wt, col0, n_cols, name):
    T, K = xn.shape
    tm, tn = 1024, 1024
    assert col0 % 8 == 0
    return pl.pallas_call(
        _proj_kernel,
        grid=(n_cols // tn, T // tm),
        in_specs=[pl.BlockSpec((tm, K), lambda j, i: (i, 0)),
                  pl.BlockSpec((pl.Element(tn), pl.Element(K)),
                               lambda j, i: (pl.multiple_of(col0 + j * tn, 8), 0))],
        out_specs=pl.BlockSpec((tm, tn), lambda j, i: (i, j)),
        out_shape=jax.ShapeDtypeStruct((T, n_cols), BF16),
        scratch_shapes=[pltpu.VMEM((tn, K), BF16)],
        compiler_params=_cparams(("arbitrary", "arbitrary")),
        name=name,
    )(xn, wt)


GLA_ROWS = 256


def _gla_kernel(q_ref, k_ref, v_ref, r_ref, glr_ref, wup_ref, bg_ref, ng_ref, o_ref, st_ref):
    @pl.when(pl.program_id(0) == 0)
    def _():
        st_ref[...] = jnp.zeros_like(st_ref)

    z = jnp.dot(glr_ref[...], wup_ref[...], preferred_element_type=F32,
                precision=lax.Precision.HIGHEST) + bg_ref[...]
    log_a = (jnp.minimum(z, 0.0) - jnp.log(1.0 + jnp.exp(-jnp.abs(z)))) * (1.0 / GLA_TAU)

    ri = lax.broadcasted_iota(I32, (CHUNK, CHUNK), 0)
    ci = lax.broadcasted_iota(I32, (CHUNK, CHUNK), 1)
    causal = ci <= ri
    tril = jnp.where(causal, 1.0, 0.0).astype(BF16)
    scale = GLA_DK ** -0.5
    ng = ng_ref[...]

    for c in range(GLA_ROWS // CHUNK):
        rows = slice(c * CHUNK, (c + 1) * CHUNK)
        la = log_a[rows]
        la_hi = la.astype(BF16)
        la_lo = (la - la_hi.astype(F32)).astype(BF16)
        bcum_all = (jnp.dot(tril, la_hi, preferred_element_type=F32)
                    + jnp.dot(tril, la_lo, preferred_element_type=F32))
        for h in range(GLA_HEADS):
            kcols = slice(h * GLA_DK, (h + 1) * GLA_DK)
            vcols = slice(h * GLA_DV, (h + 1) * GLA_DV)
            bcum = bcum_all[:, kcols]
            blast = bcum[CHUNK - 1:CHUNK, :]
            qc = q_ref[rows, kcols].astype(F32) * scale
            kc = k_ref[rows, kcols].astype(F32)
            q_dec = (qc * jnp.exp(bcum)).astype(BF16)
            k_inv = (kc * jnp.exp(-bcum)).astype(BF16)
            k_end = (kc * jnp.exp(blast - bcum)).astype(BF16)
            vv = v_ref[rows, vcols]
            a = lax.dot_general(q_dec, k_inv, (((1,), (1,)), ((), ())), preferred_element_type=F32)
            a = jnp.where(causal, a, 0.0).astype(BF16)
            st = st_ref[h]
            o = jnp.dot(a, vv, preferred_element_type=F32)
            o = o + lax.dot_general(q_dec, st.astype(BF16), (((1,), (1,)), ((), ())),
                                    preferred_element_type=F32)
            st_ref[h] = st * jnp.exp(blast) + lax.dot_general(
                vv, k_end, (((0,), (0,)), ((), ())), preferred_element_type=F32)
            ms = jnp.mean(o * o, axis=-1, keepdims=True)
            y = o * lax.rsqrt(ms + RMS_EPS) * ng
            rr = r_ref[rows, vcols].astype(F32)
            o_ref[rows, vcols] = (y * (rr * jax.nn.sigmoid(rr))).astype(o_ref.dtype)


def _gla(proj_a, glr, wup, bg, ng):
    T = proj_a.shape[0]
    R = GLA_ROWS
    return pl.pallas_call(
        _gla_kernel,
        grid=(T // R,),
        in_specs=[pl.BlockSpec((R, GLA_QK), lambda n: (n, 0)),
                  pl.BlockSpec((R, GLA_QK), lambda n: (n, 1)),
                  pl.BlockSpec((R, GLA_V), lambda n: (n, 1)),
                  pl.BlockSpec((R, GLA_V), lambda n: (n, 2)),
                  pl.BlockSpec((R, GLA_RANK), lambda n: (n, 0)),
                  pl.BlockSpec((GLA_RANK, GLA_QK), lambda n: (0, 0)),
                  pl.BlockSpec((1, GLA_QK), lambda n: (0, 0)),
                  pl.BlockSpec((1, GLA_DV), lambda n: (0, 0))],
        out_specs=pl.BlockSpec((R, GLA_V), lambda n: (n, 0)),
        out_shape=jax.ShapeDtypeStruct((T, GLA_V), BF16),
        scratch_shapes=[pltpu.VMEM((GLA_HEADS, GLA_DV, GLA_DK), F32)],
        compiler_params=_cparams(("arbitrary",)),
        name="gla",
    )(proj_a, proj_a, proj_a, proj_a, glr, wup, bg, ng)


ATT_QROWS = 256
ATT_PAD = ATT_LEFT * CHUNK
ATT_BAND = ATT_PAD + ATT_QROWS


ATT_EXT = 1024
ATT_GROUP = 2


def _att_bias_diagonals(rel_bias):
    m = np.arange(ATT_EXT)
    m = np.where(m < ATT_BAND, m, m - ATT_EXT)
    rel = np.clip(ATT_PAD - m, -REL_CLIP, REL_CLIP) + REL_CLIP
    return rel_bias.astype(F32)[:, rel].reshape(ATT_HEADS, 1, ATT_EXT)


def _att_kernel(q_ref, k_ref, v_ref, qg_ref, kg_ref, ext_ref, o_ref, kn_ref, vp_ref, tab_ref):
    qb = pl.program_id(1)

    @pl.when(qb == 0)
    def _():
        qc = lax.broadcasted_iota(I32, (ATT_QROWS, ATT_BAND), 0) // CHUNK
        kc = lax.broadcasted_iota(I32, (ATT_QROWS, ATT_BAND), 1) // CHUNK
        in_band = (kc >= qc) & (kc <= qc + ATT_LEFT)
        for g in range(ATT_GROUP):
            cols = slice(g * ATT_HD, (g + 1) * ATT_HD)
            ext = jnp.broadcast_to(ext_ref[g], (ATT_QROWS, ATT_EXT))
            tab = pltpu.roll(ext, 0, 1, stride=1, stride_axis=0)[:, :ATT_BAND]
            tab_ref[g] = jnp.where(in_band, tab, NEG_BIG)
            kk = k_ref[:, cols].astype(F32)
            ms = jnp.mean(kk * kk, axis=-1, keepdims=True)
            kn = kk * lax.rsqrt(ms + RMS_EPS) * kg_ref[...]
            kn_ref[g, 0:ATT_PAD, :] = jnp.zeros((ATT_PAD, ATT_HD), BF16)
            kn_ref[g, ATT_PAD:, :] = kn.astype(BF16)
            vp_ref[g, 0:ATT_PAD, :] = jnp.zeros((ATT_PAD, ATT_HD), BF16)
            vp_ref[g, ATT_PAD:, :] = v_ref[:, cols]

    start = pl.multiple_of(qb * ATT_QROWS, ATT_QROWS)
    kpos = start - ATT_PAD + lax.broadcasted_iota(I32, (1, ATT_BAND), 1)
    for g in range(ATT_GROUP):
        cols = slice(g * ATT_HD, (g + 1) * ATT_HD)
        q = q_ref[:, cols].astype(F32)
        ms = jnp.mean(q * q, axis=-1, keepdims=True)
        qn = (q * lax.rsqrt(ms + RMS_EPS) * qg_ref[...] * (ATT_HD ** -0.5)).astype(BF16)
        kb = kn_ref[g, pl.ds(start, ATT_BAND), :]
        vb = vp_ref[g, pl.ds(start, ATT_BAND), :]
        s = lax.dot_general(qn, kb, (((1,), (1,)), ((), ())), preferred_element_type=F32)
        s = s + tab_ref[g]
        s = jnp.where(kpos >= 0, s, NEG_BIG)
        m = jnp.max(s, axis=-1, keepdims=True)
        p = jnp.exp(s - m)
        l = jnp.sum(p, axis=-1, keepdims=True)
        o = jnp.dot(p.astype(BF16), vb, preferred_element_type=F32)
        o_ref[:, cols] = (o / l).astype(o_ref.dtype)


def _attention(proj_b, qg, kg, ext):
    T = proj_b.shape[0]
    R = ATT_QROWS
    GW = ATT_GROUP * ATT_HD
    NG = ATT_HEADS // ATT_GROUP
    return pl.pallas_call(
        _att_kernel,
        grid=(NG, T // R),
        in_specs=[pl.BlockSpec((R, GW), lambda h, n: (n, h)),
                  pl.BlockSpec((T, GW), lambda h, n: (0, NG + h)),
                  pl.BlockSpec((T, GW), lambda h, n: (0, 2 * NG + h)),
                  pl.BlockSpec((1, ATT_HD), lambda h, n: (0, 0)),
                  pl.BlockSpec((1, ATT_HD), lambda h, n: (0, 0)),
                  pl.BlockSpec((ATT_GROUP, 1, ATT_EXT), lambda h, n: (h, 0, 0))],
        out_specs=pl.BlockSpec((R, GW), lambda h, n: (n, h)),
        out_shape=jax.ShapeDtypeStruct((T, ATT_W), BF16),
        scratch_shapes=[pltpu.VMEM((ATT_GROUP, T + ATT_PAD, ATT_HD), BF16),
                        pltpu.VMEM((ATT_GROUP, T + ATT_PAD, ATT_HD), BF16),
                        pltpu.VMEM((ATT_GROUP, R, ATT_BAND), F32)],
        compiler_params=_cparams(("arbitrary", "arbitrary")),
        name="chunk_attention",
    )(proj_b, proj_b, proj_b, qg, kg, ext)


def _merge_kernel(yg_ref, ya_ref, wg_ref, wa_ref, gg_ref, ga_ref, bb_ref, o_ref):
    pg = jnp.dot(yg_ref[...], wg_ref[...], preferred_element_type=F32)
    pa = jnp.dot(ya_ref[...], wa_ref[...], preferred_element_type=F32)
    bb = bb_ref[...]
    sg = jax.nn.sigmoid(gg_ref[...].astype(F32) + bb[0:1, :])
    sa = jax.nn.sigmoid(ga_ref[...].astype(F32) + bb[1:2, :])
    o_ref[...] = (sg * pg + sa * pa).astype(o_ref.dtype)


def _merge(y_gla, y_att, wg, wa, proj_b, bb):
    T = y_gla.shape[0]
    tm, tn = 1024, 1024
    gate0 = 3 * ATT_W // tn
    return pl.pallas_call(
        _merge_kernel,
        grid=(D_MODEL // tn, T // tm),
        in_specs=[pl.BlockSpec((tm, GLA_V), lambda j, i: (i, 0)),
                  pl.BlockSpec((tm, ATT_W), lambda j, i: (i, 0)),
                  pl.BlockSpec((GLA_V, tn), lambda j, i: (0, j)),
                  pl.BlockSpec((ATT_W, tn), lambda j, i: (0, j)),
                  pl.BlockSpec((tm, tn), lambda j, i: (i, gate0 + j)),
                  pl.BlockSpec((tm, tn), lambda j, i: (i, gate0 + D_MODEL // tn + j)),
                  pl.BlockSpec((2, tn), lambda j, i: (0, j))],
        out_specs=pl.BlockSpec((tm, tn), lambda j, i: (i, j)),
        out_shape=jax.ShapeDtypeStruct((T, D_MODEL), BF16),
        compiler_params=_cparams(("arbitrary", "arbitrary")),
        name="merge",
    )(y_gla, y_att, wg, wa, proj_b, proj_b, bb)


ROUTE_ROWS = 512
ROUTE_SUB = 256
ROUTE_LANES = 128


def _route_kernel(m_ref, x_ref, wo_ref, g2_ref, wr_ref, br_ref,
                  h1_ref, hn_ref, idx_ref, wt_ref, rank_ref, cnt_ref, run_ref):
    @pl.when(pl.program_id(0) == 0)
    def _():
        run_ref[...] = jnp.zeros_like(run_ref)

    run = run_ref[...]
    for sub in range(ROUTE_ROWS // ROUTE_SUB):
        rows = slice(sub * ROUTE_SUB, (sub + 1) * ROUTE_SUB)
        h1 = x_ref[rows, :] + jnp.dot(m_ref[rows, :], wo_ref[...], preferred_element_type=F32)
        h1_ref[rows, :] = h1
        ms = jnp.mean(h1 * h1, axis=-1, keepdims=True)
        hn = h1 * lax.rsqrt(ms + RMS_EPS) * g2_ref[...]
        hn_ref[rows, :] = hn
        wr = wr_ref[...]
        hn_hi = hn.astype(BF16)
        hn_lo = (hn - hn_hi.astype(F32)).astype(BF16)
        wr_hi = wr.astype(BF16)
        wr_lo = (wr - wr_hi.astype(F32)).astype(BF16)
        logits = (jnp.dot(hn_hi, wr_hi, preferred_element_type=F32)
                  + jnp.dot(hn_hi, wr_lo, preferred_element_type=F32)
                  + jnp.dot(hn_lo, wr_hi, preferred_element_type=F32)) + br_ref[...]

        R = ROUTE_SUB
        lanes = lax.broadcasted_iota(I32, (R, N_EXPERTS), 1)
        work = logits
        vals, sels, idxs = [], [], []
        for _ in range(TOP_K):
            m = jnp.max(work, axis=-1, keepdims=True)
            idx = jnp.min(jnp.where(work == m, lanes, N_EXPERTS), axis=-1, keepdims=True)
            sel = lanes == idx
            vals.append(m)
            idxs.append(idx)
            sels.append(sel)
            work = jnp.where(sel, -jnp.inf, work)
        es = [jnp.exp(v - vals[0]) for v in vals]
        denom = es[0] + es[1] + es[2] + es[3]

        hot = jnp.zeros((R, N_EXPERTS), F32)
        for sel in sels:
            hot = jnp.where(sel, 1.0, hot)
        ri = lax.broadcasted_iota(I32, (R, R), 0)
        ci = lax.broadcasted_iota(I32, (R, R), 1)
        strict = jnp.where(ci < ri, 1.0, 0.0).astype(BF16)
        before = jnp.dot(strict, hot.astype(BF16), preferred_element_type=F32) + run
        run = run + jnp.sum(hot, axis=0, keepdims=True)

        ol = lax.broadcasted_iota(I32, (R, ROUTE_LANES), 1)
        idx_o = jnp.zeros((R, ROUTE_LANES), I32)
        wt_o = jnp.zeros((R, ROUTE_LANES), F32)
        rank_o = jnp.zeros((R, ROUTE_LANES), I32)
        for k in range(TOP_K):
            rk = jnp.sum(jnp.where(sels[k], before, 0.0), axis=-1, keepdims=True).astype(I32)
            idx_o = jnp.where(ol == k, idxs[k], idx_o)
            wt_o = jnp.where(ol == k, es[k] / denom, wt_o)
            rank_o = jnp.where(ol == k, rk, rank_o)
        idx_ref[rows, :] = idx_o
        wt_ref[rows, :] = wt_o
        rank_ref[rows, :] = rank_o
    run_ref[...] = run
    cnt_ref[...] = run


def _route(merged, x2, wo, g2, wr, br):
    T = x2.shape[0]
    R = ROUTE_ROWS
    row = lambda i: (i, 0)
    fixed = lambda i: (0, 0)
    return pl.pallas_call(
        _route_kernel,
        grid=(T // R,),
        in_specs=[pl.BlockSpec((R, D_MODEL), row),
                  pl.BlockSpec((R, D_MODEL), row),
                  pl.BlockSpec((D_MODEL, D_MODEL), fixed),
                  pl.BlockSpec((1, D_MODEL), fixed),
                  pl.BlockSpec((D_MODEL, N_EXPERTS), fixed),
                  pl.BlockSpec((1, N_EXPERTS), fixed)],
        out_specs=[pl.BlockSpec((R, D_MODEL), row),
                   pl.BlockSpec((R, D_MODEL), row),
                   pl.BlockSpec((R, ROUTE_LANES), row),
                   pl.BlockSpec((R, ROUTE_LANES), row),
                   pl.BlockSpec((R, ROUTE_LANES), row),
                   pl.BlockSpec((1, N_EXPERTS), fixed)],
        out_shape=[jax.ShapeDtypeStruct((T, D_MODEL), F32),
                   jax.ShapeDtypeStruct((T, D_MODEL), F32),
                   jax.ShapeDtypeStruct((T, ROUTE_LANES), I32),
                   jax.ShapeDtypeStruct((T, ROUTE_LANES), F32),
                   jax.ShapeDtypeStruct((T, ROUTE_LANES), I32),
                   jax.ShapeDtypeStruct((1, N_EXPERTS), F32)],
        scratch_shapes=[pltpu.VMEM((1, N_EXPERTS), F32)],
        compiler_params=_cparams(("arbitrary",)),
        name="outproj_route",
    )(merged, x2, wo, g2, wr, br)


DISP_ROWS = 1024
DMA_UNROLL = 8


def _dispatch_kernel(pos_ref, hn_ref, xs_ref, sem):
    def row_copy(g, tt, k):
        dst = pos_ref[0, 0, (g * DMA_UNROLL + tt) * TOP_K + k]
        return pltpu.make_async_copy(hn_ref.at[g, pl.ds(tt, 1), :], xs_ref.at[pl.ds(dst, 1), :], sem)

    def issue(g, carry):
        for tt in range(DMA_UNROLL):
            for k in range(TOP_K):
                row_copy(g, tt, k).start(priority=k % 2)
        return carry

    lax.fori_loop(0, DISP_ROWS // DMA_UNROLL, issue, 0)

    def drain(g, carry):
        for tt in range(DMA_UNROLL):
            for k in range(TOP_K):
                row_copy(g, tt, k).wait()
        return carry

    lax.fori_loop(0, DISP_ROWS // DMA_UNROLL, drain, 0)


def _dispatch(hn, pos, n_alloc):
    T = hn.shape[0]
    R = DISP_ROWS
    G = DMA_UNROLL
    pos3 = pos.reshape(T // R, 1, R * TOP_K)
    return pl.pallas_call(
        _dispatch_kernel,
        grid=(T // R,),
        in_specs=[pl.BlockSpec((1, 1, R * TOP_K), lambda i: (i, 0, 0), memory_space=pltpu.SMEM),
                  pl.BlockSpec((R // G, G, D_MODEL), lambda i: (i, 0, 0))],
        out_specs=pl.BlockSpec(memory_space=pl.ANY),
        out_shape=jax.ShapeDtypeStruct((n_alloc, D_MODEL), F32),
        scratch_shapes=[pltpu.SemaphoreType.DMA(())],
        compiler_params=_cparams(("arbitrary",)),
        name="dispatch",
    )(pos3, hn.reshape(T // G, G, D_MODEL))


def _expert_kernel(ue_ref, ur_ref, un_ref, xs_ref, wg_ref, wu_ref, wd_ref, bg_ref, bu_ref, bd_ref,
                   ys_ref, xb_ref, h_ref, stage_ref, ybuf_ref, pend_ref, xsem, ysem):
    u = pl.program_id(0)
    s = pl.program_id(1)
    nu = pl.num_programs(0)
    nblk = un_ref[u]
    row0 = ur_ref[u]
    slot = u % 2

    def blk_rows(b):
        return pl.ds(pl.multiple_of(b * ROW_BLK, ROW_BLK), ROW_BLK)

    def for_row_blocks(fn):
        def pair(i, carry):
            fn(pl.ds(pl.multiple_of(i * (2 * ROW_BLK), 2 * ROW_BLK), 2 * ROW_BLK))
            return carry

        lax.fori_loop(0, lax.shift_right_logical(nblk, 1), pair, 0)

        @pl.when((nblk & 1) == 1)
        def _():
            fn(blk_rows(nblk - 1))

    def x_copy(unit, b):
        rows = pl.ds(pl.multiple_of(ur_ref[unit] + b * ROW_BLK, ROW_BLK), ROW_BLK)
        return pltpu.make_async_copy(xs_ref.at[rows, :], stage_ref, xsem)

    def y_copy(b, n, ys):
        rows = pl.ds(pl.multiple_of(row0 + b * ROW_BLK, ROW_BLK), ROW_BLK)
        cols = pl.ds(pl.multiple_of(n * FF_TILE, FF_TILE), FF_TILE)
        return pltpu.make_async_copy(ybuf_ref.at[ys, blk_rows(b), :], ys_ref.at[rows, cols], ysem)

    def drain_stores():
        def wait_one(i, carry):
            y_copy(0, 0, 0).wait()
            return carry

        lax.fori_loop(0, pend_ref[0], wait_one, 0)
        pend_ref[0] = 0

    @pl.when((u == 0) & (s == 0))
    def _():
        pend_ref[0] = 0

        def load(b, carry):
            cp = x_copy(0, b)
            cp.start()
            cp.wait()
            xb_ref[0, blk_rows(b), :] = stage_ref[...].astype(BF16)
            return carry

        lax.fori_loop(0, nblk, load, 0)

    nxt = jnp.minimum(u + 1, nu - 1)
    prefetch = (u + 1 < nu) & (s < un_ref[nxt])

    @pl.when(prefetch)
    def _():
        x_copy(nxt, s).start()

    @pl.when(s < NJ)
    def _():
        wg = wg_ref[...].astype(BF16)
        wu = wu_ref[...].astype(BF16)
        bg = bg_ref[...]
        bu = bu_ref[...]

        def up_rows(rows):
            xb = xb_ref[slot, rows, :]
            g = jnp.minimum(jnp.dot(xb, wg, preferred_element_type=F32) + bg, SWIGLU_LIMIT)
            up = jnp.clip(jnp.dot(xb, wu, preferred_element_type=F32) + bu,
                          -SWIGLU_LIMIT, SWIGLU_LIMIT)
            h_ref[s, rows, :] = ((up + 1.0) * (g * jax.nn.sigmoid(SWIGLU_ALPHA * g))).astype(BF16)

        for_row_blocks(up_rows)

    @pl.when(s >= NJ)
    def _():
        n = s - NJ
        ys = n % 2
        wd = wd_ref[...].astype(BF16)
        bd = bd_ref[...]

        def down_rows(rows):
            hb = jnp.concatenate([h_ref[t, rows, :] for t in range(NJ)], axis=1)
            ybuf_ref[ys, rows, :] = jnp.dot(hb, wd, preferred_element_type=F32) + bd

        for_row_blocks(down_rows)
        drain_stores()

        def issue(b, carry):
            y_copy(b, n, ys).start()
            return carry

        lax.fori_loop(0, nblk, issue, 0)
        pend_ref[0] = nblk

    @pl.when(prefetch)
    def _():
        x_copy(nxt, s).wait()
        xb_ref[1 - slot, blk_rows(s), :] = stage_ref[...].astype(BF16)

    @pl.when((u == nu - 1) & (s == 2 * NJ - 1))
    def _():
        drain_stores()


def _experts(xs, unit_e, unit_row0, unit_nblk, n_live, w_gate, w_up, w_down, b_gate, b_up, b_down):
    n_alloc = xs.shape[0]
    n_units = unit_e.shape[0]
    last = NJ - 1

    def up_tile(u, s, ue, ur, un):
        return (ue[u], 0, jnp.where(un[u] > 0, jnp.minimum(s, last), last))

    def down_tile(u, s, ue, ur, un):
        return (ue[u], 0, jnp.where(un[u] > 0, jnp.maximum(s - NJ, 0), last))

    grid_spec = pltpu.PrefetchScalarGridSpec(
        num_scalar_prefetch=3,
        grid=(n_live, 2 * NJ),
        in_specs=[
            pl.BlockSpec(memory_space=pl.ANY),
            pl.BlockSpec((None, D_MODEL, FF_TILE), up_tile),
            pl.BlockSpec((None, D_MODEL, FF_TILE), up_tile),
            pl.BlockSpec((None, D_FF, FF_TILE), down_tile),
            pl.BlockSpec((None, 1, FF_TILE), up_tile),
            pl.BlockSpec((None, 1, FF_TILE), up_tile),
            pl.BlockSpec((None, 1, FF_TILE), down_tile),
        ],
        out_specs=pl.BlockSpec(memory_space=pl.ANY),
        scratch_shapes=[pltpu.VMEM((2, SEG_ROWS, D_MODEL), BF16),
                        pltpu.VMEM((NJ, SEG_ROWS, FF_TILE), BF16),
                        pltpu.VMEM((ROW_BLK, D_MODEL), F32),
                        pltpu.VMEM((2, SEG_ROWS, FF_TILE), F32),
                        pltpu.SMEM((1,), I32),
                        pltpu.SemaphoreType.DMA(()),
                        pltpu.SemaphoreType.DMA(())],
    )
    return pl.pallas_call(
        _expert_kernel,
        grid_spec=grid_spec,
        out_shape=jax.ShapeDtypeStruct((n_alloc, D_MODEL), F32),
        compiler_params=_cparams(("arbitrary", "arbitrary")),
        name="experts",
    )(unit_e, unit_row0, unit_nblk, xs, w_gate, w_up, w_down,
      b_gate.reshape(N_EXPERTS, 1, D_FF), b_up.reshape(N_EXPERTS, 1, D_FF),
      b_down.reshape(N_EXPERTS, 1, D_MODEL))


COMB_ROWS = 256


def _combine_kernel(pos_ref, nxt_ref, h1_ref, wt_ref, ys_ref, o_ref, buf_ref, sems):
    i = pl.program_id(0)
    n = pl.num_programs(0)
    slot = i % 2

    def row_copy(p_ref, s, g, tt, k):
        src = p_ref[0, 0, (g * DMA_UNROLL + tt) * TOP_K + k]
        return pltpu.make_async_copy(ys_ref.at[pl.ds(src, 1), :],
                                     buf_ref.at[s, k, g, pl.ds(tt, 1), :], sems.at[s])

    def issue_all(p_ref, s):
        def issue(g, carry):
            for tt in range(DMA_UNROLL):
                for k in range(TOP_K):
                    row_copy(p_ref, s, g, tt, k).start(priority=k % 2)
            return carry

        lax.fori_loop(0, COMB_ROWS // DMA_UNROLL, issue, 0)

    @pl.when(i == 0)
    def _():
        issue_all(pos_ref, 0)

    @pl.when(i + 1 < n)
    def _():
        issue_all(nxt_ref, 1 - slot)

    def drain(g, carry):
        for tt in range(DMA_UNROLL):
            for k in range(TOP_K):
                row_copy(pos_ref, slot, g, tt, k).wait()
        return carry

    lax.fori_loop(0, COMB_ROWS // DMA_UNROLL, drain, 0)

    wt = wt_ref[...]
    acc = h1_ref[...]
    for k in range(TOP_K):
        acc = acc + wt[:, :, k:k + 1] * buf_ref[slot, k]
    o_ref[...] = acc


def _combine(h1, wt, ys, pos):
    T = h1.shape[0]
    R = COMB_ROWS
    G = DMA_UNROLL
    nsteps = T // R
    pos3 = pos.reshape(nsteps, 1, R * TOP_K)
    tile = lambda w: pl.BlockSpec((R // G, G, w), lambda i: (i, 0, 0))
    out = pl.pallas_call(
        _combine_kernel,
        grid=(nsteps,),
        in_specs=[pl.BlockSpec((1, 1, R * TOP_K), lambda i: (i, 0, 0), memory_space=pltpu.SMEM),
                  pl.BlockSpec((1, 1, R * TOP_K), lambda i: (jnp.minimum(i + 1, nsteps - 1), 0, 0),
                               memory_space=pltpu.SMEM),
                  tile(D_MODEL),
                  tile(ROUTE_LANES),
                  pl.BlockSpec(memory_space=pl.ANY)],
        out_specs=tile(D_MODEL),
        out_shape=jax.ShapeDtypeStruct((T // G, G, D_MODEL), F32),
        scratch_shapes=[pltpu.VMEM((2, TOP_K, R // G, G, D_MODEL), F32),
                        pltpu.SemaphoreType.DMA((2,))],
        compiler_params=_cparams(("arbitrary",)),
        name="combine",
    )(pos3, pos3, h1.reshape(T // G, G, D_MODEL), wt.reshape(T // G, G, ROUTE_LANES), ys)
    return out.reshape(T, D_MODEL)


def _max_units(T):
    max_blocks = (T * TOP_K) // ROW_BLK + N_EXPERTS
    return (max_blocks + N_EXPERTS * (SEG_BLKS - 1)) // SEG_BLKS


def _routing_tables(counts, idx, rank, T):
    counts = counts.astype(I32)
    nblk = (counts + ROW_BLK - 1) // ROW_BLK
    pstart = (jnp.cumsum(nblk) - nblk) * ROW_BLK
    hot = idx[..., None] == jnp.arange(N_EXPERTS, dtype=I32)
    pos = jnp.sum(jnp.where(hot, pstart, 0), axis=-1) + rank
    units = (nblk + SEG_BLKS - 1) // SEG_BLKS
    uend = jnp.cumsum(units)
    ustart = uend - units
    n_units = _max_units(T)
    u = jnp.arange(n_units, dtype=I32)
    total = uend[-1]
    ue = jnp.minimum(jnp.searchsorted(uend, u, side="right").astype(I32), N_EXPERTS - 1)
    local = u - jnp.take(ustart, ue)
    un = jnp.clip(jnp.take(nblk, ue) - local * SEG_BLKS, 0, SEG_BLKS)
    ur = jnp.take(pstart, ue) + local * SEG_ROWS
    live = u < total
    last_e = jnp.take(ue, jnp.maximum(total - 1, 0))
    ue = jnp.where(live, ue, last_e)
    un = jnp.where(live, un, 0)
    ur = jnp.where(live, ur, 0)
    return pos.astype(I32), ue, ur.astype(I32), un.astype(I32), total.astype(I32)


def kernel(x, norm1_g, w_in, w_gla_gate_up, b_gla_gate, gla_norm_g, q_norm_g, k_norm_g, rel_bias,
           b_branch_gate, w_branch_gla, w_branch_att, w_out, norm2_g, w_router, b_router,
           w_gate, b_gate, w_up, b_up, w_down, b_down):
    B, S, D = x.shape
    T = B * S
    h = x.reshape(T, D)
    wt_in = jnp.swapaxes(w_in, 1, 2)
    for l in range(w_in.shape[0]):
        xn, glr = _norm1(h, norm1_g[l].reshape(1, D), wt_in[l, COL_LR:COL_B, :].astype(BF16))
        proj_a = _proj(xn, wt_in[l], 0, COLS_A, "inproj_gla")
        proj_b = _proj(xn, wt_in[l], COL_B, COLS_B, "inproj_att")
        y_gla = _gla(proj_a, glr, w_gla_gate_up[l], b_gla_gate[l].reshape(1, GLA_QK),
                     gla_norm_g[l].reshape(1, GLA_DV))
        y_att = _attention(proj_b, q_norm_g[l].reshape(1, ATT_HD), k_norm_g[l].reshape(1, ATT_HD),
                           _att_bias_diagonals(rel_bias[l]))
        merged = _merge(y_gla, y_att, w_branch_gla[l].astype(BF16), w_branch_att[l].astype(BF16),
                        proj_b, b_branch_gate[l])
        h1, hn, idx, wt, rank, counts = _route(
            merged, h, w_out[l].astype(BF16), norm2_g[l].reshape(1, D), w_router[l],
            b_router[l].reshape(1, N_EXPERTS))
        pos, ue, ur, un, n_live = _routing_tables(counts[0], idx[:, :TOP_K], rank[:, :TOP_K], T)
        n_alloc = ((T * TOP_K) // ROW_BLK + N_EXPERTS) * ROW_BLK
        xs = _dispatch(hn, pos.reshape(-1), n_alloc)
        ys = _experts(xs, ue, ur, un, n_live, w_gate[l], w_up[l], w_down[l], b_gate[l], b_up[l], b_down[l])
        h = _combine(h1, wt, ys, pos.reshape(-1))
    return h.reshape(B, S, D)
```
